```python
import math
import jax, jax.numpy as jnp
from jax import lax
import numpy as np

D_MODEL = 1024
BATCH = 16
SEQ = 256
DEPTH = 2
DEC_BATCH = 8
DEC_SEQ = 1024
PAST_LEN = 256

GRID_W = 64
D_MIX = D_MODEL
ATT_WIDTH = D_MIX // 4
SSD_WIDTH = 3 * D_MIX // 8
RWKV_WIDTH = D_MIX - ATT_WIDTH - SSD_WIDTH
SSD_HEAD_DIM = 64
SSD_HEADS = SSD_WIDTH // SSD_HEAD_DIM
SSD_GROUPS = 2
SSD_STATE = 64
SSD_CONV = 5
SSD_CHUNK = 128
SSD_CONV_DIM = SSD_WIDTH + 2 * SSD_GROUPS * SSD_STATE
RWKV_HEAD_DIM = 64
RWKV_HEADS = RWKV_WIDTH // RWKV_HEAD_DIM
DECAY_LORA = 64
AAA_LORA = 64
GATE_LORA = 128
RWKV_GN_EPS = 64e-5
ATT_V_DIM = 64
ATT_HEADS = ATT_WIDTH // ATT_V_DIM
ATT_QK_DIM = ATT_V_DIM // 2
ATT_Q_BLOCK = 128
ROPE_BASE = 10000.0
N_GROUPS = 4
EXPERTS_PER_GROUP = 4
N_EXPERTS = N_GROUPS * EXPERTS_PER_GROUP
EXPERT_HIDDEN = D_MODEL // 4
NORM_EPS = 1e-6

IN_WIDTHS = (SSD_CONV_DIM, SSD_WIDTH, 2 * SSD_HEADS,
             RWKV_WIDTH, RWKV_WIDTH, RWKV_WIDTH, DECAY_LORA, AAA_LORA, GATE_LORA,
             ATT_HEADS * 2 * ATT_QK_DIM, ATT_HEADS * 2 * ATT_QK_DIM, ATT_HEADS * ATT_V_DIM)
N_IN = sum(IN_WIDTHS)
IN_SPLITS = tuple(sum(IN_WIDTHS[:i + 1]) for i in range(len(IN_WIDTHS) - 1))

kernel_name = 'hybrid_ssd_rwkv7_diffattn_hmoe_prefix_dit_step'

F32 = jnp.float32


def rmsnorm(x, g, eps=NORM_EPS):
    xf = x.astype(F32)
    y = xf * lax.rsqrt(jnp.mean(xf * xf, axis=-1, keepdims=True) + eps)
    return (y * g.astype(F32)).astype(x.dtype)


def dwconv_centred(x, w, b):
    K = w.shape[0]
    y = lax.conv_general_dilated(x, w[:, None, :].astype(x.dtype), window_strides=(1,),
                                 padding=[(K // 2, K // 2)],
                                 dimension_numbers=('NWC', 'WIO', 'NWC'),
                                 feature_group_count=x.shape[-1])
    return y + b


def segsum(a):
    T = a.shape[-1]
    ae = jnp.broadcast_to(a[..., :, None], a.shape + (T,))
    ae = jnp.where(jnp.tril(jnp.ones((T, T), bool), -1), ae, 0.0)
    ss = jnp.cumsum(ae, axis=-2)
    return jnp.where(jnp.tril(jnp.ones((T, T), bool), 0), ss, -jnp.inf)


def ssd_scan(X, dA, Bh, Ch, init):
    b, s, h, p = X.shape
    n = Bh.shape[-1]
    q = SSD_CHUNK if s % SSD_CHUNK == 0 else s
    c = s // q
    dt = X.dtype
    Xc = X.reshape(b, c, q, h, p)
    Bc = Bh.reshape(b, c, q, h, n)
    Cc = Ch.reshape(b, c, q, h, n)
    A = jnp.transpose(dA.astype(F32).reshape(b, c, q, h), (0, 3, 1, 2))
    A_cs = jnp.cumsum(A, axis=-1)
    L = jnp.exp(segsum(A)).astype(dt)
    y_diag = jnp.einsum('bclhn,bcshn,bhcls,bcshp->bclhp', Cc, Bc, L, Xc)
    decay_states = jnp.exp(A_cs[..., -1:] - A_cs).astype(dt)
    states = jnp.einsum('bclhn,bhcl,bclhp->bchpn', Bc, decay_states, Xc)
    states = jnp.concatenate([init.astype(dt)[:, None], states], axis=1)
    chunk_tot = jnp.pad(A_cs[..., -1], ((0, 0), (0, 0), (1, 0)))
    decay_chunk = jnp.exp(segsum(chunk_tot)).astype(dt)
    new_states = jnp.einsum('bhzc,bchpn->bzhpn', decay_chunk, states)
    states, final = new_states[:, :-1], new_states[:, -1]
    y_off = jnp.einsum('bclhn,bchpn,bhcl->bclhp', Cc, states, jnp.exp(A_cs).astype(dt))
    return (y_diag + y_off).reshape(b, s, h, p), final


def ssd_branch(xbc, z, dt_raw, lp, init):
    xbc = jax.nn.silu(dwconv_centred(xbc, lp['ssd_conv_w'], lp['ssd_conv_b']))
    xs, bm, cm = jnp.split(xbc, [SSD_WIDTH, SSD_WIDTH + SSD_GROUPS * SSD_STATE], axis=-1)
    b, s, _ = xs.shape
    rep = SSD_HEADS // SSD_GROUPS
    xh = xs.reshape(b, s, SSD_HEADS, SSD_HEAD_DIM)
    bh = jnp.repeat(bm.reshape(b, s, SSD_GROUPS, SSD_STATE), rep, axis=2)
    ch = jnp.repeat(cm.reshape(b, s, SSD_GROUPS, SSD_STATE), rep, axis=2)
    dt = jax.nn.softplus(dt_raw.reshape(b, s, 2, SSD_HEADS) + lp['ssd_dt_bias'])
    a = -jnp.exp(lp['ssd_a_log'])
    rev = lambda t: jnp.flip(t, axis=1)
    y_f, s_f = ssd_scan(xh * dt[:, :, 0, :, None], dt[:, :, 0] * a[0], bh, ch, init[:, 0])
    y_b, s_b = ssd_scan(rev(xh * dt[:, :, 1, :, None]), rev(dt[:, :, 1] * a[1]), rev(bh), rev(ch), init[:, 1])
    y = y_f + rev(y_b) + lp['ssd_d'][:, None] * xh
    y = y.reshape(b, s, SSD_WIDTH) * jax.nn.silu(z)
    return rmsnorm(y, lp['ssd_norm_g']), jnp.stack([s_f, s_b], axis=1)


def rwkv_scan(r, w, k, v, kk, kka, s0, reverse):
    dt = s0.dtype
    xs = tuple(jnp.moveaxis(t.astype(dt), 1, 0) for t in (r, w, k, v, kk, kka))

    def step(S, inp):
        r_t, w_t, k_t, v_t, kk_t, kka_t = inp
        sa = jnp.einsum('bhvk,bhk->bhv', S, kk_t)
        S = (S * w_t[:, :, None, :] - sa[..., None] * kka_t[:, :, None, :]
             + v_t[..., None] * k_t[:, :, None, :])
        return S, jnp.einsum('bhvk,bhk->bhv', S, r_t)

    s_final, ys = lax.scan(step, s0, xs, reverse=reverse)
    return jnp.moveaxis(ys, 0, 1), s_final


def rwkv_branch(r, k, v, xw, xa, xg, lp, init):
    b, s, _ = r.shape
    heads = lambda t: t.reshape(b, s, RWKV_HEADS, RWKV_HEAD_DIM)
    g = jax.nn.sigmoid(xg) @ lp['rwkv_g2']
    kk = heads(k * lp['rwkv_kk']).astype(F32)
    kk = (kk * lax.rsqrt(jnp.maximum(jnp.sum(kk * kk, axis=-1, keepdims=True), 1e-24))).astype(k.dtype)
    tw = jnp.tanh(xw)
    ys, finals = [], []
    for d in range(2):
        wl = lp['rwkv_w0'][d] + tw @ lp['rwkv_w2'][d]
        decay = jnp.exp(-jnp.exp(-jax.nn.softplus(-wl) - 0.5))
        a = jax.nn.sigmoid(lp['rwkv_a0'][d] + xa @ lp['rwkv_a2'][d])
        kd = k * (1.0 + (a - 1.0) * lp['rwkv_ka'])
        y, sf = rwkv_scan(heads(r), heads(decay), heads(kd), heads(v), kk, kk * heads(a),
                          init[:, d], reverse=(d == 1))
        ys.append(y)
        finals.append(sf)
    yf = (ys[0] + ys[1]).astype(F32)
    mu = jnp.mean(yf, axis=-1, keepdims=True)
    var = jnp.mean(jnp.square(yf - mu), axis=-1, keepdims=True)
    yn = ((yf - mu) * lax.rsqrt(var + RWKV_GN_EPS)).reshape(b, s, RWKV_WIDTH)
    yn = (yn * lp['rwkv_ln_g'].astype(F32) + lp['rwkv_ln_b'].astype(F32)).astype(r.dtype)
    bonus = jnp.sum(heads(r) * heads(k) * lp['rwkv_rk'], axis=-1, keepdims=True) * heads(v)
    y = (yn + bonus.reshape(b, s, RWKV_WIDTH)) * g
    return y, jnp.stack(finals, axis=1)


def rope_1d(x, pos):
    half = x.shape[-1] // 2
    inv = ROPE_BASE ** (-jnp.arange(half, dtype=F32) / half)
    ang = pos[:, None] * inv[None, :]
    cos = jnp.cos(ang)[None, :, None, None, :].astype(x.dtype)
    sin = jnp.sin(ang)[None, :, None, None, :].astype(x.dtype)
    x1, x2 = x[..., :half], x[..., half:]
    return jnp.concatenate([x1 * cos - x2 * sin, x1 * sin + x2 * cos], axis=-1)


def rope_2d(x, row, col):
    half = x.shape[-1] // 2
    return jnp.concatenate([rope_1d(x[..., :half], row), rope_1d(x[..., half:], col)], axis=-1)


def grid_positions(n_tokens):
    rows = n_tokens // GRID_W
    row = jnp.repeat(jnp.arange(rows, dtype=F32), GRID_W)
    col = jnp.tile(jnp.arange(GRID_W, dtype=F32), rows)
    return row, col


def diff_attend(q, k, v, lam):
    b, s, h, m, d = q.shape
    blk = ATT_Q_BLOCK if s % ATT_Q_BLOCK == 0 else s
    nb = s // blk
    scale = d ** -0.5
    qb = jnp.moveaxis(q.reshape(b, nb, blk, h, m, d), 1, 0)

    def one(qblk):
        sc = jnp.einsum('bqhmd,bkhmd->bhmqk', qblk, k).astype(F32) * scale
        pr = jax.nn.softmax(sc, axis=-1)
        wgt = (pr[:, :, 0] - lam * pr[:, :, 1]).astype(v.dtype)
        return jnp.einsum('bhqk,bkhd->bqhd', wgt, v)

    o = lax.map(one, qb)
    return jnp.moveaxis(o, 0, 1).reshape(b, s, h, v.shape[-1])


def diff_attn_branch(q, k, v, lp, layer, pos, ctx_k, ctx_v):
    b, s, _ = q.shape
    q = rmsnorm(q.reshape(b, s, ATT_HEADS, 2, ATT_QK_DIM), lp['att_qnorm_g'])
    k = rmsnorm(k.reshape(b, s, ATT_HEADS, 2, ATT_QK_DIM), lp['att_knorm_g'])
    v = v.reshape(b, s, ATT_HEADS, ATT_V_DIM)
    if pos is not None:
        q = rope_2d(q, pos[0], pos[1])
        k = rope_2d(k, pos[0], pos[1])
    if ctx_k is None:
        k_all, v_all = k, v
    else:
        k_all = jnp.concatenate([ctx_k.astype(k.dtype), k], axis=1)
        v_all = jnp.concatenate([ctx_v.astype(v.dtype), v], axis=1)
    lam_init = 0.8 - 0.6 * math.exp(-0.3 * layer)
    lv = lp['att_lambda'].astype(F32)
    lam = jnp.exp(jnp.sum(lv[0] * lv[1])) - jnp.exp(jnp.sum(lv[2] * lv[3])) + lam_init
    o = diff_attend(q, k_all, v_all, lam)
    o = rmsnorm(o, lp['att_onorm_g']) * (1.0 - lam_init)
    return o.reshape(b, s, ATT_WIDTH), k, v


def hier_moe(x, lp):
    b, s, d = x.shape
    t = x.reshape(b * s, d)
    gp = jax.nn.softmax((t @ lp['moe_wg'] + lp['moe_bg']).astype(F32), axis=-1)
    g_val, g_idx = lax.top_k(gp, 1)
    el = (t @ lp['moe_we'] + lp['moe_be']).astype(F32).reshape(-1, N_GROUPS, EXPERTS_PER_GROUP)
    el = jnp.take_along_axis(el, g_idx[:, :, None], axis=1)[:, 0]
    e_val, e_idx = lax.top_k(jax.nn.softmax(el, axis=-1), 2)
    e_val = e_val / jnp.sum(e_val, axis=-1, keepdims=True)
    eid = g_idx * EXPERTS_PER_GROUP + e_idx
    gate = jnp.sum(jax.nn.one_hot(eid, N_EXPERTS, dtype=F32) * (g_val * e_val)[..., None], axis=1).astype(x.dtype)
    hid = jax.nn.silu(jnp.einsum('td,edf->tef', t, lp['moe_w1'])) * jnp.einsum('td,edf->tef', t, lp['moe_w3'])
    y = jnp.einsum('tef,efd->td', hid * gate[:, :, None], lp['moe_w2'])
    return y.reshape(b, s, d)


def trunk_layer(x, cond, lp, layer, pos, ssd_init, rwkv_init, ctx_k, ctx_v):
    mod = (jax.nn.silu(cond) @ lp['ada_w'] + lp['ada_b']).reshape(-1, 1, 6 * D_MODEL)
    sh1, sc1, g1, sh2, sc2, g2 = jnp.split(mod, 6, axis=-1)
    h = rmsnorm(x, lp['norm1_g']) * (1.0 + sc1) + sh1
    (xbc, z, dt_raw, r, k, v, xw, xa, xg, qa, ka, va) = jnp.split(h @ lp['w_in'], IN_SPLITS, axis=-1)
    y_ssd, st_ssd = ssd_branch(xbc, z, dt_raw, lp, ssd_init)
    y_rwkv, st_rwkv = rwkv_branch(r, k, v, xw, xa, xg, lp, rwkv_init)
    y_att, k_att, v_att = diff_attn_branch(qa, ka, va, lp, layer, pos, ctx_k, ctx_v)
    x = x + g1 * (jnp.concatenate([y_ssd, y_rwkv, y_att], axis=-1) @ lp['w_out'])
    h = rmsnorm(x, lp['norm2_g']) * (1.0 + sc2) + sh2
    x = x + g2 * hier_moe(h, lp)
    return x, k_att, v_att, st_ssd, st_rwkv


def setup_inputs(seed: int = 0) -> dict:
    key = jax.random.key(seed)
    keys = iter(jax.random.split(key, 64))
    L = DEPTH

    def nrm(shape, scale):
        return jax.random.normal(next(keys), shape, F32) * scale

    dt0 = jnp.exp(jax.random.uniform(next(keys), (L, 2, SSD_HEADS), F32)
                  * (math.log(0.1) - math.log(0.001)) + math.log(0.001))
    return {
        'x_prompt': nrm((BATCH, SEQ, D_MODEL), 1.0),
        'x_sample': nrm((DEC_BATCH, DEC_SEQ, D_MODEL), 1.0),
        'c': nrm((DEC_BATCH, D_MODEL), 1.0),
        'cache_attn_k': nrm((DEC_BATCH, L, PAST_LEN, ATT_HEADS, 2, ATT_QK_DIM), 1.0),
        'cache_attn_v': nrm((DEC_BATCH, L, PAST_LEN, ATT_HEADS, ATT_V_DIM), 1.0),
        'state_ssd': nrm((DEC_BATCH, L, 2, SSD_HEADS, SSD_HEAD_DIM, SSD_STATE), 0.3),
        'state_rwkv': nrm((DEC_BATCH, L, 2, RWKV_HEADS, RWKV_HEAD_DIM, RWKV_HEAD_DIM), 0.3),
        'c_ctx': nrm((D_MODEL,), 1.0),
        'ada_w': nrm((L, D_MODEL, 6 * D_MODEL), 0.5 * D_MODEL ** -0.5),
        'ada_b': nrm((L, 6 * D_MODEL), 0.02),
        'norm1_g': 1.0 + nrm((L, D_MODEL), 0.02),
        'norm2_g': 1.0 + nrm((L, D_MODEL), 0.02),
        'w_in': nrm((L, D_MODEL, N_IN), D_MODEL ** -0.5),
        'ssd_conv_w': nrm((L, SSD_CONV, SSD_CONV_DIM), SSD_CONV ** -0.5),
        'ssd_conv_b': nrm((L, SSD_CONV_DIM), 0.02),
        'ssd_dt_bias': dt0 + jnp.log(-jnp.expm1(-dt0)),
        'ssd_a_log': jnp.log(jax.random.uniform(next(keys), (L, 2, SSD_HEADS), F32, 1.0, 16.0)),
        'ssd_d': 1.0 + nrm((L, SSD_HEADS), 0.1),
        'ssd_norm_g': 1.0 + nrm((L, SSD_WIDTH), 0.02),
        'rwkv_w0': nrm((L, 2, RWKV_WIDTH), 1.0),
        'rwkv_w2': nrm((L, 2, DECAY_LORA, RWKV_WIDTH), 0.1),
        'rwkv_a0': nrm((L, 2, RWKV_WIDTH), 0.5),
        'rwkv_a2': nrm((L, 2, AAA_LORA, RWKV_WIDTH), AAA_LORA ** -0.5),
        'rwkv_g2': nrm((L, GATE_LORA, RWKV_WIDTH), GATE_LORA ** -0.5),
        'rwkv_kk': 0.85 + nrm((L, RWKV_WIDTH), 0.05),
        'rwkv_ka': 1.0 + nrm((L, RWKV_WIDTH), 0.05),
        'rwkv_rk': nrm((L, RWKV_HEADS, RWKV_HEAD_DIM), 0.1),
        'rwkv_ln_g': 1.0 + nrm((L, RWKV_WIDTH), 0.02),
        'rwkv_ln_b': nrm((L, RWKV_WIDTH), 0.02),
        'att_qnorm_g': 1.0 + nrm((L, ATT_QK_DIM), 0.02),
        'att_knorm_g': 1.0 + nrm((L, ATT_QK_DIM), 0.02),
        'att_lambda': nrm((L, 4, ATT_QK_DIM), 0.1),
        'att_onorm_g': 1.0 + nrm((L, ATT_V_DIM), 0.02),
        'w_out': nrm((L, D_MIX, D_MODEL), D_MIX ** -0.5),
        'moe_wg': nrm((L, D_MODEL, N_GROUPS), D_MODEL ** -0.5),
        'moe_bg': nrm((L, N_GROUPS), 0.01),
        'moe_we': nrm((L, D_MODEL, N_EXPERTS), D_MODEL ** -0.5),
        'moe_be': nrm((L, N_EXPERTS), 0.01),
        'moe_w1': nrm((L, N_EXPERTS, D_MODEL, EXPERT_HIDDEN), D_MODEL ** -0.5),
        'moe_w3': nrm((L, N_EXPERTS, D_MODEL, EXPERT_HIDDEN), D_MODEL ** -0.5),
        'moe_w2': nrm((L, N_EXPERTS, EXPERT_HIDDEN, D_MODEL), EXPERT_HIDDEN ** -0.5),
    }


def reference(x_prompt, x_sample, c, cache_attn_k, cache_attn_v, state_ssd, state_rwkv, c_ctx,
              ada_w, ada_b, norm1_g, norm2_g, w_in, ssd_conv_w, ssd_conv_b, ssd_dt_bias, ssd_a_log,
              ssd_d, ssd_norm_g, rwkv_w0, rwkv_w2, rwkv_a0, rwkv_a2, rwkv_g2, rwkv_kk, rwkv_ka, rwkv_rk,
              rwkv_ln_g, rwkv_ln_b, att_qnorm_g, att_knorm_g, att_lambda, att_onorm_g, w_out,
              moe_wg, moe_bg, moe_we, moe_be, moe_w1, moe_w3, moe_w2):
    yp, ys = x_prompt, x_sample
    bp = x_prompt.shape[0]
    pos = grid_positions(x_sample.shape[1])
    new_k, new_v, new_ssd, new_rwkv = [], [], [], []
    for l in range(DEPTH):
        lp = {
            'ada_w': ada_w[l], 'ada_b': ada_b[l], 'norm1_g': norm1_g[l], 'norm2_g': norm2_g[l],
            'w_in': w_in[l], 'ssd_conv_w': ssd_conv_w[l], 'ssd_conv_b': ssd_conv_b[l],
            'ssd_dt_bias': ssd_dt_bias[l], 'ssd_a_log': ssd_a_log[l], 'ssd_d': ssd_d[l],
            'ssd_norm_g': ssd_norm_g[l], 'rwkv_w0': rwkv_w0[l], 'rwkv_w2': rwkv_w2[l],
            'rwkv_a0': rwkv_a0[l], 'rwkv_a2': rwkv_a2[l], 'rwkv_g2': rwkv_g2[l], 'rwkv_kk': rwkv_kk[l],
            'rwkv_ka': rwkv_ka[l], 'rwkv_rk': rwkv_rk[l], 'rwkv_ln_g': rwkv_ln_g[l], 'rwkv_ln_b': rwkv_ln_b[l],
            'att_qnorm_g': att_qnorm_g[l], 'att_knorm_g': att_knorm_g[l], 'att_lambda': att_lambda[l],
            'att_onorm_g': att_onorm_g[l], 'w_out': w_out[l], 'moe_wg': moe_wg[l], 'moe_bg': moe_bg[l],
            'moe_we': moe_we[l], 'moe_be': moe_be[l], 'moe_w1': moe_w1[l], 'moe_w3': moe_w3[l],
            'moe_w2': moe_w2[l],
        }
        zero_ssd = jnp.zeros((bp, 2, SSD_HEADS, SSD_HEAD_DIM, SSD_STATE), x_prompt.dtype)
        zero_rwkv = jnp.zeros((bp, 2, RWKV_HEADS, RWKV_HEAD_DIM, RWKV_HEAD_DIM), x_prompt.dtype)
        yp, k_l, v_l, s_l, r_l = trunk_layer(yp, c_ctx, lp, l, None, zero_ssd, zero_rwkv, None, None)
        new_k.append(k_l)
        new_v.append(v_l)
        new_ssd.append(s_l)
        new_rwkv.append(r_l)
        ys, _, _, _, _ = trunk_layer(ys, c, lp, l, pos, state_ssd[:, l], state_rwkv[:, l],
                                     cache_attn_k[:, l], cache_attn_v[:, l])
    return (yp, ys, jnp.stack(new_k, axis=1), jnp.stack(new_v, axis=1),
            jnp.stack(new_ssd, axis=1), jnp.stack(new_rwkv, axis=1))
```

```python
import functools
import math

import jax
import jax.numpy as jnp
from jax import lax
from jax.experimental import pallas as pl
from jax.experimental.pallas import tpu as pltpu

F32 = jnp.float32
BF16 = jnp.bfloat16
HIGHEST = lax.Precision.HIGHEST

D_MODEL = 1024
DEPTH = 2
GRID_W = 64
SSD_WIDTH = 384
SSD_HEADS = 6
SSD_GROUPS = 2
SSD_STATE = 64
SSD_HEAD_DIM = 64
SSD_CONV = 5
SSD_CONV_DIM = 640
SSD_CHUNK = 128
RWKV_WIDTH = 384
RWKV_HEADS = 6
RWKV_HEAD_DIM = 64
RWKV_CHUNK = 64
RWKV_SUB = 8
RWKV_GN_EPS = 64e-5
ATT_WIDTH = 256
ATT_HEADS = 4
ATT_QK_DIM = 32
ATT_V_DIM = 64
ATT_Q_BLOCK = 256
ROPE_BASE = 10000.0
N_GROUPS = 4
EXPERTS_PER_GROUP = 4
N_EXPERTS = 16
EXPERT_HIDDEN = 256
NORM_EPS = 1e-6

LANES = 128
SUBLANES = 8
TOKEN_TILE = 256
VMEM_LIMIT_BYTES = 56 * 1024 * 1024

P_WIDTH = 3328
P_Z, P_R, P_K, P_V = 0, 1, 2, 3
P_LORA, P_QA, P_KA, P_VA = 6, 7, 8, 9
P_XBC = 4
P_DT = 25
MOD_ROWS = 16


def _dot(a, b, precision=None):
    return lax.dot_general(a, b, (((1,), (0,)), ((), ())), precision=precision, preferred_element_type=F32)


def _dot_nt(a, b, precision=None):
    return lax.dot_general(a, b, (((1,), (1,)), ((), ())), precision=precision, preferred_element_type=F32)


def _dot_tn(a, b, precision=None):
    return lax.dot_general(a, b, (((0,), (0,)), ((), ())), precision=precision, preferred_element_type=F32)


def _sigmoid(x):
    return 1.0 / (1.0 + jnp.exp(-x))


def _silu(x):
    return x * _sigmoid(x)


def _softplus(x):
    return jnp.maximum(x, 0.0) + jnp.log1p(jnp.exp(-jnp.abs(x)))


def _block_diag_ones(n, blk):
    r = lax.broadcasted_iota(jnp.int32, (n, n), 0) // blk
    c = lax.broadcasted_iota(jnp.int32, (n, n), 1) // blk
    return (r == c).astype(F32)


def _const_spec(shape):
    zeros = (0,) * len(shape)
    return pl.BlockSpec(shape, lambda *_: zeros, pipeline_mode=pl.Buffered(1))


def _params(n_axes=1):
    return pltpu.CompilerParams(dimension_semantics=("arbitrary",) * n_axes, vmem_limit_bytes=VMEM_LIMIT_BYTES)


def _ada_kernel(c_ref, w_ref, b_ref, o_ref):
    o_ref[0] = _dot(_silu(c_ref[...]), w_ref[0], HIGHEST) + b_ref[0]


def _ada_mod(cond, ada_w, ada_b):
    tn = 512
    n = 6 * D_MODEL
    return pl.pallas_call(
        _ada_kernel,
        out_shape=jax.ShapeDtypeStruct((DEPTH, MOD_ROWS, n), F32),
        grid=(DEPTH, n // tn),
        in_specs=[
            pl.BlockSpec((MOD_ROWS, D_MODEL), lambda l, j: (0, 0)),
            pl.BlockSpec((1, D_MODEL, tn), lambda l, j: (l, 0, j)),
            pl.BlockSpec((1, 1, tn), lambda l, j: (l, 0, j)),
        ],
        out_specs=pl.BlockSpec((1, MOD_ROWS, tn), lambda l, j: (l, 0, j)),
        compiler_params=_params(2),
        name="ada_mod",
    )(cond, ada_w, ada_b.reshape(DEPTH, 1, n))


def _modulated_norm(x, gain, shift, scale):
    y = x * lax.rsqrt(jnp.mean(x * x, axis=-1, keepdims=True) + NORM_EPS)
    return y * gain * (1.0 + scale) + shift


def _in_proj_kernel(x_ref, mod_ref, g_ref, w_ref, o_ref):
    h = _modulated_norm(x_ref[...], g_ref[...], mod_ref[0, 0:1, :], mod_ref[0, 1:2, :]).astype(BF16)
    cw = 512
    for c0 in range(0, P_WIDTH, cw):
        c1 = min(c0 + cw, P_WIDTH)
        o_ref[:, c0:c1] = _dot(h, w_ref[:, c0:c1])


def _in_proj(x, mod, gain, w, mod_row):
    n_tok = x.shape[0]
    return pl.pallas_call(
        _in_proj_kernel,
        out_shape=jax.ShapeDtypeStruct((n_tok, P_WIDTH), F32),
        grid=(n_tok // TOKEN_TILE,),
        in_specs=[
            pl.BlockSpec((TOKEN_TILE, D_MODEL), lambda i: (i, 0)),
            pl.BlockSpec((1, 6, D_MODEL), lambda i: (mod_row(i), 0, 0)),
            _const_spec((1, D_MODEL)),
            _const_spec((D_MODEL, P_WIDTH)),
        ],
        out_specs=pl.BlockSpec((TOKEN_TILE, P_WIDTH), lambda i: (i, 0)),
        compiler_params=_params(1),
        name="in_proj",
    )(x, mod, gain, w)


def _ssd_kernel(xbc_ref, z_ref, dt_ref, cw_ref, vec_ref, dtp_ref, init_ref, y_ref, fin_ref,
                pad_ref, xc_ref, dt_s, da_s, yacc_ref, st_ref):
    seq = xbc_ref.shape[0]
    q = SSD_CHUNK
    n_chunks = seq // q
    pad_ref[0:SUBLANES, :] = jnp.zeros((SUBLANES, SSD_CONV_DIM), F32)
    pad_ref[seq + SUBLANES:seq + 2 * SUBLANES, :] = jnp.zeros((SUBLANES, SSD_CONV_DIM), F32)
    pad_ref[SUBLANES:seq + SUBLANES, :] = xbc_ref[...]
    acc = jnp.zeros((seq, SSD_CONV_DIM), F32) + cw_ref[SSD_CONV:SSD_CONV + 1, :]
    for j in range(SSD_CONV):
        off = SUBLANES + j - SSD_CONV // 2
        acc = acc + cw_ref[j:j + 1, :] * pad_ref[off:off + seq, :]
    xc_ref[...] = _silu(acc)
    dt = _softplus(dt_ref[...] + dtp_ref[0:1, :])
    dt_s[...] = dt
    da_s[...] = dt * (-jnp.exp(dtp_ref[1:2, :]))

    row = lax.broadcasted_iota(jnp.int32, (q, q), 0)
    col = lax.broadcasted_iota(jnp.int32, (q, q), 1)
    rep = SSD_HEADS // SSD_GROUPS
    bc0 = SSD_WIDTH
    cc0 = SSD_WIDTH + SSD_GROUPS * SSD_STATE

    for d in range(2):
        reverse = d == 1
        keep = (col >= row) if reverse else (col <= row)
        tri = keep.astype(F32)
        for h in range(SSD_HEADS):
            st_ref[h] = init_ref[0, d, h]

        def chunk_body(i, carry, d=d, reverse=reverse, keep=keep, tri=tri):
            c = (n_chunks - 1 - i) if reverse else i
            rows = pl.ds(pl.multiple_of(c * q, q), q)
            da = da_s[rows, :]
            g = _dot(tri, da, HIGHEST)
            g_t = g.T
            tot = jnp.sum(da, axis=0, keepdims=True)
            dtc = dt_s[rows, :]
            xcc = xc_ref[rows, :]
            cb = []
            for grp in range(SSD_GROUPS):
                bg = xcc[:, bc0 + grp * SSD_STATE:bc0 + (grp + 1) * SSD_STATE]
                cg = xcc[:, cc0 + grp * SSD_STATE:cc0 + (grp + 1) * SSD_STATE]
                cb.append((bg, cg, _dot_nt(cg.astype(BF16), bg.astype(BF16))))
            ys = []
            for h in range(SSD_HEADS):
                cidx = d * SSD_HEADS + h
                bg, cg, cbm = cb[h // rep]
                g_col = g[:, cidx:cidx + 1]
                g_row = g_t[cidx:cidx + 1, :]
                lm = jnp.exp(jnp.where(keep, g_col - g_row, -jnp.inf))
                xh = xcc[:, h * SSD_HEAD_DIM:(h + 1) * SSD_HEAD_DIM] * dtc[:, cidx:cidx + 1]
                st = st_ref[h]
                y = _dot((cbm * lm).astype(BF16), xh.astype(BF16))
                y = y + _dot_nt(cg.astype(BF16), st.astype(BF16)) * jnp.exp(g_col)
                tot_h = tot[:, cidx:cidx + 1]
                xdec = xh * jnp.exp(tot_h - g_col)
                st_ref[h] = jnp.exp(tot_h) * st + _dot_tn(xdec.astype(BF16), bg.astype(BF16))
                ys.append(y)
            ycat = jnp.concatenate(ys, axis=1)
            if reverse:
                yacc_ref[rows, :] = yacc_ref[rows, :] + ycat
            else:
                yacc_ref[rows, :] = ycat
            return carry

        lax.fori_loop(0, n_chunks, chunk_body, 0)
        for h in range(SSD_HEADS):
            fin_ref[0, d, h] = st_ref[h]

    y = yacc_ref[...] + vec_ref[0:1, :] * xc_ref[:, 0:SSD_WIDTH]
    y = y * _silu(z_ref[...])
    y = y * lax.rsqrt(jnp.mean(y * y, axis=-1, keepdims=True) + NORM_EPS)
    y_ref[...] = y * vec_ref[1:2, :]


def _ssd_branch(p, seq, n_batch, conv, vec, dtp, init):
    blk = seq // TOKEN_TILE
    del blk
    st_shape = (n_batch, 2, SSD_HEADS, SSD_HEAD_DIM, SSD_STATE)
    return pl.pallas_call(
        _ssd_kernel,
        out_shape=(jax.ShapeDtypeStruct((n_batch * seq, SSD_WIDTH), F32), jax.ShapeDtypeStruct(st_shape, F32)),
        grid=(n_batch,),
        in_specs=[
            pl.BlockSpec((seq, SSD_CONV_DIM), lambda b: (b, P_XBC)),
            pl.BlockSpec((seq, SSD_WIDTH), lambda b: (b, P_Z)),
            pl.BlockSpec((seq, LANES), lambda b: (b, P_DT)),
            _const_spec((SUBLANES, SSD_CONV_DIM)),
            _const_spec((SUBLANES, SSD_WIDTH)),
            _const_spec((SUBLANES, LANES)),
            pl.BlockSpec((1,) + st_shape[1:], lambda b: (b, 0, 0, 0, 0)),
        ],
        out_specs=(
            pl.BlockSpec((seq, SSD_WIDTH), lambda b: (b, 0)),
            pl.BlockSpec((1,) + st_shape[1:], lambda b: (b, 0, 0, 0, 0)),
        ),
        scratch_shapes=[
            pltpu.VMEM((seq + 2 * SUBLANES, SSD_CONV_DIM), F32),
            pltpu.VMEM((seq, SSD_CONV_DIM), F32),
            pltpu.VMEM((seq, LANES), F32),
            pltpu.VMEM((seq, LANES), F32),
            pltpu.VMEM((seq, SSD_WIDTH), F32),
            pltpu.VMEM((SSD_HEADS, SSD_HEAD_DIM, SSD_STATE), F32),
        ],
        compiler_params=_params(1),
        name="ssd",
    )(p, p, p, conv, vec, dtp, init)


def _rwkv_solve(nb, w, sub_diag):
    n = nb.shape[0]
    eye = (lax.broadcasted_iota(jnp.int32, (n, n), 0) == lax.broadcasted_iota(jnp.int32, (n, n), 1)).astype(F32)
    nd = jnp.where(sub_diag, nb, 0.0)
    no = nb - nd
    nd2 = _dot(nd, nd, HIGHEST)
    nd4 = _dot(nd2, nd2, HIGHEST)
    ldinv = eye - nd
    ldinv = ldinv + _dot(ldinv, nd2, HIGHEST)
    ldinv = ldinv + _dot(ldinv, nd4, HIGHEST)
    m = _dot(ldinv, no, HIGHEST)
    x = _dot(ldinv, w, HIGHEST)
    m2 = _dot(m, m, HIGHEST)
    m4 = _dot(m2, m2, HIGHEST)
    x = x - _dot(m, x, HIGHEST)
    x = x + _dot(m2, x, HIGHEST)
    x = x + _dot(m4, x, HIGHEST)
    return x


def _rwkv_kernel(r_ref, k_ref, v_ref, lora_ref, vec_ref, w2_ref, a2_ref, g2_ref, init_ref, y_ref, fin_ref,
                 kkn_s, lw_s, kd_s, kka_s, yacc_ref, st_ref):
    seq = r_ref.shape[0]
    t = RWKV_CHUNK
    n_chunks = seq // t
    hd = RWKV_HEAD_DIM
    r = r_ref[...]
    k = k_ref[...]
    lora = lora_ref[...]
    xw = lora[:, 0:64]
    xa = lora[:, 64:128]
    xg = lora[:, 128:256]
    bd = _block_diag_ones(RWKV_WIDTH, hd)
    kkp = k * vec_ref[0:1, :]
    kss = _dot(kkp * kkp, bd, HIGHEST)
    kkn_s[...] = kkp * lax.rsqrt(jnp.maximum(kss, 1e-24))
    tw = jnp.tanh(xw)

    n2 = 2 * t
    ri = lax.broadcasted_iota(jnp.int32, (n2, n2), 0)
    ci = lax.broadcasted_iota(jnp.int32, (n2, n2), 1)
    rt = ri % t
    ct = ci % t
    top = ri < t
    row_is_top = lax.broadcasted_iota(jnp.int32, (n2, hd), 0) < t
    tr = lax.broadcasted_iota(jnp.int32, (t, t), 0)
    tc = lax.broadcasted_iota(jnp.int32, (t, t), 1)
    zeros_tv = jnp.zeros((t, hd), F32)

    for d in range(2):
        reverse = d == 1
        wl = vec_ref[5 + d:6 + d, :] + _dot(tw, w2_ref[d], HIGHEST)
        lw_s[...] = -math.exp(-0.5) * _sigmoid(wl)
        a = _sigmoid(vec_ref[7 + d:8 + d, :] + _dot(xa, a2_ref[d], HIGHEST))
        kd_s[...] = k * (1.0 + (a - 1.0) * vec_ref[1:2, :])
        kka_s[...] = kkn_s[...] * a
        if reverse:
            strict = ct > rt
            incl = ct >= rt
            tri = (tc >= tr).astype(F32)
        else:
            strict = ct < rt
            incl = ct <= rt
            tri = (tc <= tr).astype(F32)
        keep = (top & strict) | (~top & incl)
        solve_blk = top & (ci < t)
        sub_diag = (ri // RWKV_SUB) == (ci // RWKV_SUB)
        for h in range(RWKV_HEADS):
            st_ref[h] = init_ref[0, d, h]

        def chunk_body(i, carry, reverse=reverse, keep=keep, solve_blk=solve_blk, sub_diag=sub_diag, tri=tri):
            c = (n_chunks - 1 - i) if reverse else i
            rows = pl.ds(pl.multiple_of(c * t, t), t)
            lw = lw_s[rows, :]
            g = _dot(tri, lw, HIGHEST)
            e_g = jnp.exp(g)
            e_ng = jnp.exp(-g)
            kap = kkn_s[rows, :] * jnp.exp(g - lw)
            rr = r_ref[rows, :] * e_g
            bb = kka_s[rows, :] * e_ng
            kk = kd_s[rows, :] * e_ng
            vv = v_ref[rows, :]
            e_tot = jnp.exp(jnp.sum(lw, axis=0, keepdims=True))
            ys = []
            for h in range(RWKV_HEADS):
                sl = slice(h * hd, (h + 1) * hd)
                lq = jnp.concatenate([kap[:, sl], rr[:, sl]], axis=0)
                rq = jnp.concatenate([bb[:, sl], kk[:, sl]], axis=0)
                vh = vv[:, sl]
                s0 = st_ref[h]
                mm = jnp.where(keep, _dot_nt(lq, rq, HIGHEST), 0.0)
                c0 = _dot_nt(lq, s0, HIGHEST)
                z_v = jnp.concatenate([zeros_tv, vh], axis=0)
                mv = _dot(mm, z_v, HIGHEST)
                w_full = jnp.where(row_is_top, c0 + mv, 0.0)
                x = _rwkv_solve(jnp.where(solve_blk, mm, 0.0), w_full, sub_diag)
                z_u = -x
                mu = _dot(mm, z_u, HIGHEST)
                ys.append((c0 + mv + mu)[t:, :])
                st_ref[h] = (s0 + _dot_tn(z_u + z_v, rq, HIGHEST)) * e_tot[:, sl]
            ycat = jnp.concatenate(ys, axis=1)
            if reverse:
                yacc_ref[rows, :] = yacc_ref[rows, :] + ycat
            else:
                yacc_ref[rows, :] = ycat
            return carry

        lax.fori_loop(0, n_chunks, chunk_body, 0)
        for h in range(RWKV_HEADS):
            fin_ref[0, d, h] = st_ref[h]

    yf = yacc_ref[...]
    inv = 1.0 / hd
    mu = _dot(yf, bd, HIGHEST) * inv
    yc = yf - mu
    var = _dot(yc * yc, bd, HIGHEST) * inv
    yn = yc * lax.rsqrt(var + RWKV_GN_EPS) * vec_ref[3:4, :] + vec_ref[4:5, :]
    v = v_ref[...]
    bonus = _dot(r * k * vec_ref[2:3, :], bd, HIGHEST) * v
    gate = _dot(_sigmoid(xg).astype(BF16), g2_ref[...].astype(BF16))
    y_ref[...] = (yn + bonus) * gate


def _rwkv_branch(p, seq, n_batch, vec, w2, a2, g2, init):
    st_shape = (n_batch, 2, RWKV_HEADS, RWKV_HEAD_DIM, RWKV_HEAD_DIM)
    wide = lambda col: pl.BlockSpec((seq, RWKV_WIDTH), lambda b: (b, col))
    return pl.pallas_call(
        _rwkv_kernel,
        out_shape=(jax.ShapeDtypeStruct((n_batch * seq, RWKV_WIDTH), F32), jax.ShapeDtypeStruct(st_shape, F32)),
        grid=(n_batch,),
        in_specs=[
            wide(P_R), wide(P_K), wide(P_V),
            pl.BlockSpec((seq, 256), lambda b: (b, P_LORA)),
            _const_spec((16, RWKV_WIDTH)),
            _const_spec((2, 64, RWKV_WIDTH)),
            _const_spec((2, 64, RWKV_WIDTH)),
            _const_spec((128, RWKV_WIDTH)),
            pl.BlockSpec((1,) + st_shape[1:], lambda b: (b, 0, 0, 0, 0)),
        ],
        out_specs=(
            pl.BlockSpec((seq, RWKV_WIDTH), lambda b: (b, 0)),
            pl.BlockSpec((1,) + st_shape[1:], lambda b: (b, 0, 0, 0, 0)),
        ),
        scratch_shapes=[pltpu.VMEM((seq, RWKV_WIDTH), F32)] * 5
        + [pltpu.VMEM((RWKV_HEADS, RWKV_HEAD_DIM, RWKV_HEAD_DIM), F32)],
        compiler_params=_params(1),
        name="rwkv",
    )(p, p, p, p, vec, w2, a2, g2, init)


def _group_rmsnorm(x, bd, width, gain):
    ms = _dot(x * x, bd, HIGHEST) * (1.0 / width)
    return x * lax.rsqrt(ms + NORM_EPS) * gain


def _rope(x, tab_ref):
    half = 8
    x_next = pltpu.roll(x, ATT_WIDTH - half, 1)
    x_prev = pltpu.roll(x, half, 1)
    return x * tab_ref[0] + x_next * tab_ref[1] + x_prev * tab_ref[2]


def _attn_kernel(*refs, layer, latent):
    if latent:
        (q_ref, k_ref, v_ref, gv_ref, lam_ref, tab_ref, ck_ref, cv_ref, y_ref, ka_s, va_s, qn_s) = refs
    else:
        (q_ref, k_ref, v_ref, gv_ref, lam_ref, y_ref, kn_ref, vo_ref, ka_s, va_s, qn_s) = refs
    seq = q_ref.shape[0]
    bd32 = _block_diag_ones(ATT_WIDTH, ATT_QK_DIM)
    qn = _group_rmsnorm(q_ref[...], bd32, ATT_QK_DIM, gv_ref[0:1, :])
    kn = _group_rmsnorm(k_ref[...], bd32, ATT_QK_DIM, gv_ref[1:2, :])
    v = v_ref[...]
    if latent:
        past = ck_ref.shape[2]
        qn = _rope(qn, tab_ref)
        kn = _rope(kn, tab_ref)
        ka_s[0:past, :] = ck_ref[0, 0].astype(BF16)
        va_s[0:past, :] = cv_ref[0, 0].astype(BF16)
    else:
        past = 0
        kn_ref[...] = kn
        vo_ref[...] = v
    ka_s[past:past + seq, :] = kn.astype(BF16)
    va_s[past:past + seq, :] = v.astype(BF16)
    qn_s[...] = qn

    lam_init = 0.8 - 0.6 * math.exp(-0.3 * layer)
    lv = lam_ref[...]
    lam = (jnp.exp(jnp.sum(lv[0:1, :] * lv[1:2, :], axis=-1, keepdims=True))
           - jnp.exp(jnp.sum(lv[2:3, :] * lv[3:4, :], axis=-1, keepdims=True)) + lam_init)
    scale = ATT_QK_DIM ** -0.5
    bd64 = _block_diag_ones(LANES, ATT_V_DIM)
    lane = lax.broadcasted_iota(jnp.int32, (1, LANES), 1)
    tq = min(ATT_Q_BLOCK, seq)

    def q_block(i, carry):
        rows = pl.ds(pl.multiple_of(i * tq, tq), tq)
        outs = []
        for pair in range(ATT_HEADS // 2):
            cols = slice(pair * LANES, (pair + 1) * LANES)
            q_pair = qn_s[rows, cols]
            k_pair = ka_s[:, cols]
            v_pair = va_s[:, cols]
            o_heads = []
            for hh in range(2):
                probs = []
                for m in range(2):
                    lo = hh * 2 * ATT_QK_DIM + m * ATT_QK_DIM
                    sel = (lane >= lo) & (lane < lo + ATT_QK_DIM)
                    qm = jnp.where(sel, q_pair, 0.0).astype(BF16)
                    sc = _dot_nt(qm, k_pair) * scale
                    sc = sc - jnp.max(sc, axis=-1, keepdims=True)
                    e = jnp.exp(sc)
                    probs.append(e / jnp.sum(e, axis=-1, keepdims=True))
                wgt = probs[0] - lam * probs[1]
                o_heads.append(_dot(wgt.astype(BF16), v_pair))
            o = jnp.where(lane < ATT_V_DIM, o_heads[0], o_heads[1])
            ms = _dot(o * o, bd64, HIGHEST) * (1.0 / ATT_V_DIM)
            outs.append(o * lax.rsqrt(ms + NORM_EPS) * gv_ref[2:3, cols] * (1.0 - lam_init))
        y_ref[rows, :] = jnp.concatenate(outs, axis=1)
        return carry

    lax.fori_loop(0, seq // tq, q_block, 0)


def _attn_branch(p, seq, n_batch, layer, gv, lam, rope_tab=None, cache_k=None, cache_v=None):
    latent = cache_k is not None
    blk = lambda col: pl.BlockSpec((seq, ATT_WIDTH), lambda b: (b, col))
    in_specs = [blk(P_QA), blk(P_KA), blk(P_VA), _const_spec((SUBLANES, ATT_WIDTH)), _const_spec((4, ATT_QK_DIM))]
    args = [p, p, p, gv, lam]
    seq_out = jax.ShapeDtypeStruct((n_batch * seq, ATT_WIDTH), F32)
    out_blk = pl.BlockSpec((seq, ATT_WIDTH), lambda b: (b, 0))
    if latent:
        past = cache_k.shape[2]
        in_specs += [
            _const_spec((3, seq, ATT_WIDTH)),
            pl.BlockSpec((1, 1, past, ATT_WIDTH), lambda b: (b, layer, 0, 0)),
            pl.BlockSpec((1, 1, past, ATT_WIDTH), lambda b: (b, layer, 0, 0)),
        ]
        args += [rope_tab, cache_k, cache_v]
        out_shape, out_specs = seq_out, out_blk
    else:
        past = 0
        out_shape, out_specs = (seq_out,) * 3, (out_blk,) * 3
    return pl.pallas_call(
        functools.partial(_attn_kernel, layer=layer, latent=latent),
        out_shape=out_shape,
        grid=(n_batch,),
        in_specs=in_specs,
        out_specs=out_specs,
        scratch_shapes=[
            pltpu.VMEM((past + seq, ATT_WIDTH), BF16),
            pltpu.VMEM((past + seq, ATT_WIDTH), BF16),
            pltpu.VMEM((seq, ATT_WIDTH), F32),
        ],
        compiler_params=_params(1),
        name="attn",
    )(*args)


def _route(logits):
    lane = lax.broadcasted_iota(jnp.int32, logits.shape, 1)
    big = jnp.int32(1 << 20)
    neg = -jnp.inf
    gmask = lane < N_GROUPS
    gl = jnp.where(gmask, logits, neg)
    gmax = jnp.max(gl, axis=-1, keepdims=True)
    gsum = jnp.sum(jnp.where(gmask, jnp.exp(gl - gmax), 0.0), axis=-1, keepdims=True)
    g_val = 1.0 / gsum
    g_idx = jnp.min(jnp.where(gmask & (gl == gmax), lane, big), axis=-1, keepdims=True)
    e_lane = lane - N_GROUPS
    sel = (e_lane >= 0) & (e_lane < N_EXPERTS) & ((e_lane // EXPERTS_PER_GROUP) == g_idx)
    l1 = jnp.max(jnp.where(sel, logits, neg), axis=-1, keepdims=True)
    i1 = jnp.min(jnp.where(sel & (logits == l1), lane, big), axis=-1, keepdims=True)
    sel2 = sel & (lane != i1)
    l2 = jnp.max(jnp.where(sel2, logits, neg), axis=-1, keepdims=True)
    i2 = jnp.min(jnp.where(sel2 & (logits == l2), lane, big), axis=-1, keepdims=True)
    e2 = jnp.exp(l2 - l1)
    w1 = 1.0 / (1.0 + e2)
    w2 = e2 * w1
    return jnp.where(lane == i1, g_val * w1, 0.0) + jnp.where(lane == i2, g_val * w2, 0.0)


def _post_kernel(x_ref, ys_ref, yr_ref, ya_ref, mod_ref, g_ref, wo_ref, wr_ref, br_ref, w1_ref, w3_ref, w2_ref, o_ref):
    mix = _dot(ys_ref[...].astype(BF16), wo_ref[0:SSD_WIDTH, :])
    mix = mix + _dot(yr_ref[...].astype(BF16), wo_ref[SSD_WIDTH:SSD_WIDTH + RWKV_WIDTH, :])
    mix = mix + _dot(ya_ref[...].astype(BF16), wo_ref[SSD_WIDTH + RWKV_WIDTH:, :])
    x = x_ref[...] + mod_ref[0, 2:3, :] * mix
    h = _modulated_norm(x, g_ref[...], mod_ref[0, 3:4, :], mod_ref[0, 4:5, :])
    gate = _route(_dot(h, wr_ref[...], HIGHEST) + br_ref[...])
    hb = h.astype(BF16)
    acc = jnp.zeros(x.shape, F32)
    for e in range(N_EXPERTS):
        a = _dot(hb, w1_ref[e])
        b = _dot(hb, w3_ref[e])
        hid = _silu(a) * b * gate[:, N_GROUPS + e:N_GROUPS + e + 1]
        acc = acc + _dot(hid.astype(BF16), w2_ref[e])
    o_ref[...] = x + mod_ref[0, 5:6, :] * acc


def _post(x, y_ssd, y_rwkv, y_att, mod, gain, wo, wr, br, w1, w3, w2, mod_row):
    n_tok = x.shape[0]
    tok = lambda width: pl.BlockSpec((TOKEN_TILE, width), lambda i: (i, 0))
    return pl.pallas_call(
        _post_kernel,
        out_shape=jax.ShapeDtypeStruct((n_tok, D_MODEL), F32),
        grid=(n_tok // TOKEN_TILE,),
        in_specs=[
            tok(D_MODEL), tok(SSD_WIDTH), tok(RWKV_WIDTH), tok(ATT_WIDTH),
            pl.BlockSpec((1, 6, D_MODEL), lambda i: (mod_row(i), 0, 0)),
            _const_spec((1, D_MODEL)),
            _const_spec((D_MODEL, D_MODEL)),
            _const_spec((D_MODEL, LANES)),
            _const_spec((1, LANES)),
            _const_spec((N_EXPERTS, D_MODEL, EXPERT_HIDDEN)),
            _const_spec((N_EXPERTS, D_MODEL, EXPERT_HIDDEN)),
            _const_spec((N_EXPERTS, EXPERT_HIDDEN, D_MODEL)),
        ],
        out_specs=tok(D_MODEL),
        compiler_params=_params(1),
        name="post",
    )(x, y_ssd, y_rwkv, y_att, mod, gain, wo, wr, br, w1, w3, w2)


def _pad_rows(a, rows):
    return jnp.pad(a, ((0, rows - a.shape[0]), (0, 0)))


def _reorder_w_in(w_in):
    widths = (640, 384, 12, 384, 384, 384, 64, 64, 128, 256, 256, 256)
    offs = [0]
    for w in widths:
        offs.append(offs[-1] + w)
    g = [w_in[..., offs[i]:offs[i + 1]] for i in range(len(widths))]
    xbc, z, dt, r, k, v, xw, xa, xg, qa, ka, va = g
    pad = jnp.zeros(w_in.shape[:-1] + (LANES - 12,), w_in.dtype)
    return jnp.concatenate([z, r, k, v, xw, xa, xg, qa, ka, va, xbc, dt, pad], axis=-1)


def _rope_tables(n_tokens):
    half = ATT_QK_DIM // 2
    quarter = half // 2
    tok = jnp.arange(n_tokens)
    row = (tok // GRID_W).astype(F32)
    col = (tok % GRID_W).astype(F32)
    inv = ROPE_BASE ** (-jnp.arange(quarter, dtype=F32) / quarter)
    lane = jnp.arange(ATT_WIDTH)
    within = lane % ATT_QK_DIM
    use_col = within >= half
    freq = inv[within % quarter]
    second = (within % half) >= quarter
    pos = jnp.where(use_col[None, :], col[:, None], row[:, None])
    ang = pos * freq[None, :]
    cos = jnp.cos(ang)
    sin = jnp.sin(ang)
    return jnp.stack([cos, jnp.where(second[None, :], 0.0, -sin), jnp.where(second[None, :], sin, 0.0)])


def kernel(x_prompt, x_sample, c, cache_attn_k, cache_attn_v, state_ssd, state_rwkv, c_ctx, ada_w, ada_b, norm1_g, norm2_g, w_in, ssd_conv_w, ssd_conv_b, ssd_dt_bias, ssd_a_log, ssd_d, ssd_norm_g, rwkv_w0, rwkv_w2, rwkv_a0, rwkv_a2, rwkv_g2, rwkv_kk, rwkv_ka, rwkv_rk, rwkv_ln_g, rwkv_ln_b, att_qnorm_g, att_knorm_g, att_lambda, att_onorm_g, w_out, moe_wg, moe_bg, moe_we, moe_be, moe_w1, moe_w3, moe_w2):
    n_ctx, s_ctx, _ = x_prompt.shape
    n_lat, s_lat, _ = x_sample.shape
    past = cache_attn_k.shape[2]

    cond = _pad_rows(jnp.concatenate([c_ctx[None, :], c], axis=0), MOD_ROWS)
    mod = _ada_mod(cond, ada_w, ada_b).reshape(DEPTH, MOD_ROWS, 6, D_MODEL)

    w_in_p = _reorder_w_in(w_in).astype(BF16)
    w_out_b = w_out.astype(BF16)
    w1_b, w3_b, w2_b = moe_w1.astype(BF16), moe_w3.astype(BF16), moe_w2.astype(BF16)
    w_route = jnp.concatenate(
        [moe_wg, moe_we, jnp.zeros((DEPTH, D_MODEL, LANES - N_GROUPS - N_EXPERTS), F32)], axis=-1)
    b_route = jnp.concatenate(
        [moe_bg, moe_be, jnp.zeros((DEPTH, LANES - N_GROUPS - N_EXPERTS), F32)], axis=-1)[:, None, :]
    rope_tab = _rope_tables(s_lat)
    cache_k = cache_attn_k.reshape(n_lat, DEPTH, past, ATT_WIDTH)
    cache_v = cache_attn_v.reshape(n_lat, DEPTH, past, ATT_WIDTH)
    zero_ssd = jnp.zeros((n_ctx, 2, SSD_HEADS, SSD_HEAD_DIM, SSD_STATE), F32)
    zero_rwkv = jnp.zeros((n_ctx, 2, RWKV_HEADS, RWKV_HEAD_DIM, RWKV_HEAD_DIM), F32)

    ctx_row = lambda i: 0
    lat_row = lambda i: 1 + i // (s_lat // TOKEN_TILE)

    xp = x_prompt.reshape(n_ctx * s_ctx, D_MODEL)
    xs = x_sample.reshape(n_lat * s_lat, D_MODEL)
    new_k, new_v, new_ssd, new_rwkv = [], [], [], []
    for l in range(DEPTH):
        conv = _pad_rows(jnp.concatenate([ssd_conv_w[l], ssd_conv_b[l][None, :]], axis=0), SUBLANES)
        ssd_vec = _pad_rows(jnp.stack([jnp.repeat(ssd_d[l], SSD_HEAD_DIM), ssd_norm_g[l]]), SUBLANES)
        dtp = _pad_rows(jnp.pad(jnp.stack([ssd_dt_bias[l].reshape(-1), ssd_a_log[l].reshape(-1)]),
                                ((0, 0), (0, LANES - 2 * SSD_HEADS))), SUBLANES)
        rwkv_vec = _pad_rows(jnp.stack([
            rwkv_kk[l], rwkv_ka[l], rwkv_rk[l].reshape(-1), rwkv_ln_g[l], rwkv_ln_b[l],
            rwkv_w0[l, 0], rwkv_w0[l, 1], rwkv_a0[l, 0], rwkv_a0[l, 1]]), 16)
        tile32 = lambda g: jnp.tile(g, ATT_WIDTH // ATT_QK_DIM)
        att_vec = _pad_rows(jnp.stack([tile32(att_qnorm_g[l]), tile32(att_knorm_g[l]),
                                       jnp.tile(att_onorm_g[l], ATT_WIDTH // ATT_V_DIM)]), SUBLANES)
        g1 = norm1_g[l][None, :]
        g2 = norm2_g[l][None, :]
        post_w = (mod[l], g2, w_out_b[l], w_route[l], b_route[l], w1_b[l], w3_b[l], w2_b[l])

        p = _in_proj(xp, mod[l], g1, w_in_p[l], ctx_row)
        y_ssd, st_ssd = _ssd_branch(p, s_ctx, n_ctx, conv, ssd_vec, dtp, zero_ssd)
        y_rwkv, st_rwkv = _rwkv_branch(p, s_ctx, n_ctx, rwkv_vec, rwkv_w2[l], rwkv_a2[l], rwkv_g2[l], zero_rwkv)
        y_att, k_att, v_att = _attn_branch(p, s_ctx, n_ctx, l, att_vec, att_lambda[l])
        xp = _post(xp, y_ssd, y_rwkv, y_att, *post_w, ctx_row)
        new_k.append(k_att.reshape(n_ctx, s_ctx, ATT_HEADS, 2, ATT_QK_DIM))
        new_v.append(v_att.reshape(n_ctx, s_ctx, ATT_HEADS, ATT_V_DIM))
        new_ssd.append(st_ssd)
        new_rwkv.append(st_rwkv)

        p = _in_proj(xs, mod[l], g1, w_in_p[l], lat_row)
        y_ssd, _ = _ssd_branch(p, s_lat, n_lat, conv, ssd_vec, dtp, state_ssd[:, l])
        y_rwkv, _ = _rwkv_branch(p, s_lat, n_lat, rwkv_vec, rwkv_w2[l], rwkv_a2[l], rwkv_g2[l], state_rwkv[:, l])
        y_att = _attn_branch(p, s_lat, n_lat, l, att_vec, att_lambda[l], rope_tab, cache_k, cache_v)
        xs = _post(xs, y_ssd, y_rwkv, y_att, *post_w, lat_row)

    return (xp.reshape(n_ctx, s_ctx, D_MODEL), xs.reshape(n_lat, s_lat, D_MODEL),
            jnp.stack(new_k, axis=1), jnp.stack(new_v, axis=1),
            jnp.stack(new_ssd, axis=1), jnp.stack(new_rwkv, axis=1))
```

```python
import functools
import math

import jax
import jax.numpy as jnp
from jax import lax
from jax.experimental import pallas as pl
from jax.experimental.pallas import tpu as pltpu

F32 = jnp.float32
BF16 = jnp.bfloat16
HIGHEST = lax.Precision.HIGHEST

D_MODEL = 1024
DEPTH = 2
GRID_W = 64
SSD_WIDTH = 384
SSD_HEADS = 6
SSD_GROUPS = 2
SSD_STATE = 64
SSD_HEAD_DIM = 64
SSD_CONV = 5
SSD_CONV_DIM = 640
SSD_CHUNK = 128
RWKV_WIDTH = 384
RWKV_HEADS = 6
RWKV_HEAD_DIM = 64
RWKV_CHUNK = 64
RWKV_SUB = 8
RWKV_GN_EPS = 64e-5
ATT_WIDTH = 256
ATT_HEADS = 4
ATT_QK_DIM = 32
ATT_V_DIM = 64
ATT_Q_BLOCK = 256
ROPE_BASE = 10000.0
N_GROUPS = 4
EXPERTS_PER_GROUP = 4
N_EXPERTS = 16
EXPERT_HIDDEN = 256
NORM_EPS = 1e-6

LANES = 128
SUBLANES = 8
TOKEN_TILE = 256
VMEM_LIMIT_BYTES = 56 * 1024 * 1024

P_WIDTH = 3328
P_Z, P_R, P_K, P_V = 0, 1, 2, 3
P_LORA, P_QA, P_KA, P_VA = 6, 7, 8, 9
P_XBC = 4
P_DT = 25
MOD_ROWS = 16


def _dot(a, b, precision=None):
    return lax.dot_general(a, b, (((1,), (0,)), ((), ())), precision=precision, preferred_element_type=F32)


def _dot_nt(a, b, precision=None):
    return lax.dot_general(a, b, (((1,), (1,)), ((), ())), precision=precision, preferred_element_type=F32)


def _dot_tn(a, b, precision=None):
    return lax.dot_general(a, b, (((0,), (0,)), ((), ())), precision=precision, preferred_element_type=F32)


def _bdot(a, b):
    return _dot(a.astype(BF16), b.astype(BF16))


def _bdot_nt(a, b):
    return _dot_nt(a.astype(BF16), b.astype(BF16))


def _bdot_tn(a, b):
    return _dot_tn(a.astype(BF16), b.astype(BF16))


def _bf16_terms(a, terms):
    parts = []
    for _ in range(terms):
        part = a.astype(BF16)
        a = a - part.astype(F32)
        parts.append(part)
    return parts


def _dot01(a, b01, terms=3):
    b = b01.astype(BF16)
    return sum(_dot(part, b) for part in _bf16_terms(a, terms))


def _dot01_left(a01, b, terms=3):
    a = a01.astype(BF16)
    return sum(_dot(a, part) for part in _bf16_terms(b, terms))


def _sigmoid(x):
    return 1.0 / (1.0 + jnp.exp(-x))


def _silu(x):
    return x * _sigmoid(x)


def _softplus(x):
    return jnp.maximum(x, 0.0) + jnp.log1p(jnp.exp(-jnp.abs(x)))


def _block_diag_ones(n, blk):
    r = lax.broadcasted_iota(jnp.int32, (n, n), 0) // blk
    c = lax.broadcasted_iota(jnp.int32, (n, n), 1) // blk
    return (r == c).astype(F32)


def _const_spec(shape):
    zeros = (0,) * len(shape)
    return pl.BlockSpec(shape, lambda *_: zeros, pipeline_mode=pl.Buffered(1))


def _params(n_axes=1):
    return pltpu.CompilerParams(dimension_semantics=("arbitrary",) * n_axes, vmem_limit_bytes=VMEM_LIMIT_BYTES)


def _ada_kernel(c_ref, w_ref, b_ref, o_ref):
    o_ref[0] = _dot(_silu(c_ref[...]), w_ref[0], HIGHEST) + b_ref[0]


def _ada_mod(cond, ada_w, ada_b):
    tn = 512
    n = 6 * D_MODEL
    return pl.pallas_call(
        _ada_kernel,
        out_shape=jax.ShapeDtypeStruct((DEPTH, MOD_ROWS, n), F32),
        grid=(DEPTH, n // tn),
        in_specs=[
            pl.BlockSpec((MOD_ROWS, D_MODEL), lambda l, j: (0, 0)),
            pl.BlockSpec((1, D_MODEL, tn), lambda l, j: (l, 0, j)),
            pl.BlockSpec((1, 1, tn), lambda l, j: (l, 0, j)),
        ],
        out_specs=pl.BlockSpec((1, MOD_ROWS, tn), lambda l, j: (l, 0, j)),
        compiler_params=_params(2),
        name="ada_mod",
    )(cond, ada_w, ada_b.reshape(DEPTH, 1, n))


def _modulated_norm(x, gain, shift, scale):
    y = x * lax.rsqrt(jnp.mean(x * x, axis=-1, keepdims=True) + NORM_EPS)
    return y * gain * (1.0 + scale) + shift


def _in_proj_kernel(x_ref, mod_ref, g_ref, w_ref, o_ref):
    h = _modulated_norm(x_ref[...], g_ref[...], mod_ref[0, 0:1, :], mod_ref[0, 1:2, :]).astype(BF16)
    cw = 512
    for c0 in range(0, P_WIDTH, cw):
        c1 = min(c0 + cw, P_WIDTH)
        o_ref[:, c0:c1] = _dot(h, w_ref[:, c0:c1])


def _in_proj(x, mod, gain, w, mod_row):
    n_tok = x.shape[0]
    return pl.pallas_call(
        _in_proj_kernel,
        out_shape=jax.ShapeDtypeStruct((n_tok, P_WIDTH), F32),
        grid=(n_tok // TOKEN_TILE,),
        in_specs=[
            pl.BlockSpec((TOKEN_TILE, D_MODEL), lambda i: (i, 0)),
            pl.BlockSpec((1, 6, D_MODEL), lambda i: (mod_row(i), 0, 0)),
            _const_spec((1, D_MODEL)),
            _const_spec((D_MODEL, P_WIDTH)),
        ],
        out_specs=pl.BlockSpec((TOKEN_TILE, P_WIDTH), lambda i: (i, 0)),
        compiler_params=_params(1),
        name="in_proj",
    )(x, mod, gain, w)


def _ssd_kernel(xbc_ref, z_ref, dt_ref, cw_ref, vec_ref, dtp_ref, init_ref, y_ref, fin_ref,
                pad_ref, xc_ref, dt_s, da_s, yacc_ref, st_ref):
    seq = xbc_ref.shape[0]
    q = SSD_CHUNK
    n_chunks = seq // q
    pad_ref[0:SUBLANES, :] = jnp.zeros((SUBLANES, SSD_CONV_DIM), F32)
    pad_ref[seq + SUBLANES:seq + 2 * SUBLANES, :] = jnp.zeros((SUBLANES, SSD_CONV_DIM), F32)
    pad_ref[SUBLANES:seq + SUBLANES, :] = xbc_ref[...]
    acc = jnp.zeros((seq, SSD_CONV_DIM), F32) + cw_ref[SSD_CONV:SSD_CONV + 1, :]
    for j in range(SSD_CONV):
        off = SUBLANES + j - SSD_CONV // 2
        acc = acc + cw_ref[j:j + 1, :] * pad_ref[off:off + seq, :]
    xc_ref[...] = _silu(acc)
    dt = _softplus(dt_ref[...] + dtp_ref[0:1, :])
    dt_s[...] = dt
    da_s[...] = dt * (-jnp.exp(dtp_ref[1:2, :]))

    row = lax.broadcasted_iota(jnp.int32, (q, q), 0)
    col = lax.broadcasted_iota(jnp.int32, (q, q), 1)
    rep = SSD_HEADS // SSD_GROUPS
    bc0 = SSD_WIDTH
    cc0 = SSD_WIDTH + SSD_GROUPS * SSD_STATE

    for d in range(2):
        reverse = d == 1
        keep = (col >= row) if reverse else (col <= row)
        tri = keep.astype(F32)
        for h in range(SSD_HEADS):
            st_ref[h] = init_ref[0, d, h]

        def chunk_body(i, carry, d=d, reverse=reverse, keep=keep, tri=tri):
            c = (n_chunks - 1 - i) if reverse else i
            rows = pl.ds(pl.multiple_of(c * q, q), q)
            da = da_s[rows, :]
            g = _dot01_left(tri, da)
            g_t = g.T
            tot = jnp.sum(da, axis=0, keepdims=True)
            dtc = dt_s[rows, :]
            xcc = xc_ref[rows, :]
            cb = []
            for grp in range(SSD_GROUPS):
                bg = xcc[:, bc0 + grp * SSD_STATE:bc0 + (grp + 1) * SSD_STATE]
                cg = xcc[:, cc0 + grp * SSD_STATE:cc0 + (grp + 1) * SSD_STATE]
                cb.append((bg, cg, _dot_nt(cg.astype(BF16), bg.astype(BF16))))
            ys = []
            for h in range(SSD_HEADS):
                cidx = d * SSD_HEADS + h
                bg, cg, cbm = cb[h // rep]
                g_col = g[:, cidx:cidx + 1]
                g_row = g_t[cidx:cidx + 1, :]
                lm = jnp.exp(jnp.where(keep, g_col - g_row, -jnp.inf))
                xh = xcc[:, h * SSD_HEAD_DIM:(h + 1) * SSD_HEAD_DIM] * dtc[:, cidx:cidx + 1]
                st = st_ref[h]
                y = _dot((cbm * lm).astype(BF16), xh.astype(BF16))
                y = y + _dot_nt(cg.astype(BF16), st.astype(BF16)) * jnp.exp(g_col)
                tot_h = tot[:, cidx:cidx + 1]
                xdec = xh * jnp.exp(tot_h - g_col)
                st_ref[h] = jnp.exp(tot_h) * st + _dot_tn(xdec.astype(BF16), bg.astype(BF16))
                ys.append(y)
            ycat = jnp.concatenate(ys, axis=1)
            if reverse:
                yacc_ref[rows, :] = yacc_ref[rows, :] + ycat
            else:
                yacc_ref[rows, :] = ycat
            return carry

        lax.fori_loop(0, n_chunks, chunk_body, 0)
        for h in range(SSD_HEADS):
            fin_ref[0, d, h] = st_ref[h]

    y = yacc_ref[...] + vec_ref[0:1, :] * xc_ref[:, 0:SSD_WIDTH]
    y = y * _silu(z_ref[...])
    y = y * lax.rsqrt(jnp.mean(y * y, axis=-1, keepdims=True) + NORM_EPS)
    y_ref[...] = y * vec_ref[1:2, :]


def _ssd_branch(p, seq, n_batch, conv, vec, dtp, init):
    blk = seq // TOKEN_TILE
    del blk
    st_shape = (n_batch, 2, SSD_HEADS, SSD_HEAD_DIM, SSD_STATE)
    return pl.pallas_call(
        _ssd_kernel,
        out_shape=(jax.ShapeDtypeStruct((n_batch * seq, SSD_WIDTH), F32), jax.ShapeDtypeStruct(st_shape, F32)),
        grid=(n_batch,),
        in_specs=[
            pl.BlockSpec((seq, SSD_CONV_DIM), lambda b: (b, P_XBC)),
            pl.BlockSpec((seq, SSD_WIDTH), lambda b: (b, P_Z)),
            pl.BlockSpec((seq, LANES), lambda b: (b, P_DT)),
            _const_spec((SUBLANES, SSD_CONV_DIM)),
            _const_spec((SUBLANES, SSD_WIDTH)),
            _const_spec((SUBLANES, LANES)),
            pl.BlockSpec((1,) + st_shape[1:], lambda b: (b, 0, 0, 0, 0)),
        ],
        out_specs=(
            pl.BlockSpec((seq, SSD_WIDTH), lambda b: (b, 0)),
            pl.BlockSpec((1,) + st_shape[1:], lambda b: (b, 0, 0, 0, 0)),
        ),
        scratch_shapes=[
            pltpu.VMEM((seq + 2 * SUBLANES, SSD_CONV_DIM), F32),
            pltpu.VMEM((seq, SSD_CONV_DIM), F32),
            pltpu.VMEM((seq, LANES), F32),
            pltpu.VMEM((seq, LANES), F32),
            pltpu.VMEM((seq, SSD_WIDTH), F32),
            pltpu.VMEM((SSD_HEADS, SSD_HEAD_DIM, SSD_STATE), F32),
        ],
        compiler_params=_params(1),
        name="ssd",
    )(p, p, p, conv, vec, dtp, init)


def _rwkv_kernel(r_ref, k_ref, v_ref, lora_ref, vec_ref, w2_ref, a2_ref, g2_ref, init_ref, y_ref, fin_ref,
                 kkn_s, lw_s, kd_s, kka_s, y_s, st_ref):
    seq = r_ref.shape[0]
    t = RWKV_CHUNK
    n_chunks = seq // t
    hd = RWKV_HEAD_DIM
    n2 = 2 * t
    r = r_ref[...]
    k = k_ref[...]
    lora = lora_ref[...]
    xw = lora[:, 0:64]
    xa = lora[:, 64:128]
    xg = lora[:, 128:256]
    bd = _block_diag_ones(RWKV_WIDTH, hd)
    kkp = k * vec_ref[0:1, :]
    kss = _dot01(kkp * kkp, bd)
    kkn = kkp * lax.rsqrt(jnp.maximum(kss, 1e-24))
    kkn_s[...] = kkn
    tw = jnp.tanh(xw)
    for d in range(2):
        wl = vec_ref[5 + d:6 + d, :] + _bdot(tw, w2_ref[d])
        lw_s[d] = -math.exp(-0.5) * _sigmoid(wl)
        a = _sigmoid(vec_ref[7 + d:8 + d, :] + _bdot(xa, a2_ref[d]))
        kd_s[d] = k * (1.0 + (a - 1.0) * vec_ref[1:2, :])
        kka_s[d] = kkn * a
        for h in range(RWKV_HEADS):
            st_ref[d, h] = init_ref[0, d, h]

    ri = lax.broadcasted_iota(jnp.int32, (n2, n2), 0)
    ci = lax.broadcasted_iota(jnp.int32, (n2, n2), 1)
    rt = ri % t
    ct = ci % t
    top = ri < t
    solve_blk = top & (ci < t)
    sub_diag = solve_blk & ((ri // RWKV_SUB) == (ci // RWKV_SUB))
    sub_off = solve_blk & ((ri // RWKV_SUB) != (ci // RWKV_SUB))
    row_is_top = lax.broadcasted_iota(jnp.int32, (n2, hd), 0) < t
    tr = lax.broadcasted_iota(jnp.int32, (t, t), 0)
    tc = lax.broadcasted_iota(jnp.int32, (t, t), 1)
    keep = [(top & (ct < rt)) | (~top & (ct <= rt)), (top & (ct > rt)) | (~top & (ct >= rt))]
    tri = [(tc <= tr).astype(F32), (tc >= tr).astype(F32)]
    zeros_tv = jnp.zeros((t, hd), F32)

    def chunk_body(i, carry):
        ch = []
        for d in range(2):
            c = i if d == 0 else n_chunks - 1 - i
            rows = pl.ds(pl.multiple_of(c * t, t), t)
            lw = lw_s[d, rows, :]
            g = _dot01_left(tri[d], lw)
            e_g = jnp.exp(g)
            e_ng = jnp.exp(-g)
            kap = kkn_s[rows, :] * jnp.exp(g - lw)
            rr = r_ref[rows, :] * e_g
            bb = kka_s[d, rows, :] * e_ng
            kk = kd_s[d, rows, :] * e_ng
            vv = v_ref[rows, :]
            e_tot = jnp.exp(jnp.sum(lw, axis=0, keepdims=True))
            for h in range(RWKV_HEADS):
                sl = slice(h * hd, (h + 1) * hd)
                ch.append(dict(
                    d=d, h=h, rows=rows,
                    lq=jnp.concatenate([kap[:, sl], rr[:, sl]], axis=0),
                    rq=jnp.concatenate([bb[:, sl], kk[:, sl]], axis=0),
                    zv=jnp.concatenate([zeros_tv, vv[:, sl]], axis=0),
                    s0=st_ref[d, h], etot=e_tot[:, sl]))
        for x in ch:
            x["mm"] = jnp.where(keep[x["d"]], _bdot_nt(x["lq"], x["rq"]), 0.0)
            x["c0"] = _bdot_nt(x["lq"], x["s0"])
        for x in ch:
            x["base"] = x["c0"] + _bdot(x["mm"], x["zv"])
            x["nd"] = jnp.where(sub_diag, x["mm"], 0.0)
            x["rhs"] = jnp.concatenate([jnp.where(sub_off, x["mm"], 0.0),
                                        jnp.where(row_is_top, x["base"], 0.0)], axis=1)
        for x in ch:
            x["nd2"] = _bdot(x["nd"], x["nd"])
            x["y"] = x["rhs"] - _bdot(x["nd"], x["rhs"])
        for x in ch:
            x["nd4"] = _bdot(x["nd2"], x["nd2"])
            x["y"] = x["y"] + _bdot(x["nd2"], x["y"])
        for x in ch:
            x["y"] = x["y"] + _bdot(x["nd4"], x["y"])
            x["m"] = x["y"][:, 0:n2]
            x["x"] = x["y"][:, n2:]
        for x in ch:
            x["m2"] = _bdot(x["m"], x["m"])
            x["x"] = x["x"] - _bdot(x["m"], x["x"])
        for x in ch:
            x["m4"] = _bdot(x["m2"], x["m2"])
            x["x"] = x["x"] + _bdot(x["m2"], x["x"])
        for x in ch:
            x["x"] = x["x"] + _bdot(x["m4"], x["x"])
            x["zu"] = -x["x"]
        for x in ch:
            x["yo"] = (x["base"] + _bdot(x["mm"], x["zu"]))[t:, :]
            st_ref[x["d"], x["h"]] = (x["s0"] + _bdot_tn(x["zu"] + x["zv"], x["rq"])) * x["etot"]
        for d in range(2):
            rows = ch[d * RWKV_HEADS]["rows"]
            y_s[d, rows, :] = jnp.concatenate([x["yo"] for x in ch if x["d"] == d], axis=1)
        return carry

    lax.fori_loop(0, n_chunks, chunk_body, 0)
    for d in range(2):
        for h in range(RWKV_HEADS):
            fin_ref[0, d, h] = st_ref[d, h]

    yf = y_s[0] + y_s[1]
    inv = 1.0 / hd
    mu = _dot01(yf, bd) * inv
    yc = yf - mu
    var = _dot01(yc * yc, bd) * inv
    yn = yc * lax.rsqrt(var + RWKV_GN_EPS) * vec_ref[3:4, :] + vec_ref[4:5, :]
    v = v_ref[...]
    bonus = _dot01(r * k * vec_ref[2:3, :], bd) * v
    gate = _bdot(_sigmoid(xg), g2_ref[...])
    y_ref[...] = (yn + bonus) * gate


def _rwkv_branch(p, seq, n_batch, vec, w2, a2, g2, init):
    st_shape = (n_batch, 2, RWKV_HEADS, RWKV_HEAD_DIM, RWKV_HEAD_DIM)
    wide = lambda col: pl.BlockSpec((seq, RWKV_WIDTH), lambda b: (b, col))
    return pl.pallas_call(
        _rwkv_kernel,
        out_shape=(jax.ShapeDtypeStruct((n_batch * seq, RWKV_WIDTH), F32), jax.ShapeDtypeStruct(st_shape, F32)),
        grid=(n_batch,),
        in_specs=[
            wide(P_R), wide(P_K), wide(P_V),
            pl.BlockSpec((seq, 256), lambda b: (b, P_LORA)),
            _const_spec((16, RWKV_WIDTH)),
            _const_spec((2, 64, RWKV_WIDTH)),
            _const_spec((2, 64, RWKV_WIDTH)),
            _const_spec((128, RWKV_WIDTH)),
            pl.BlockSpec((1,) + st_shape[1:], lambda b: (b, 0, 0, 0, 0)),
        ],
        out_specs=(
            pl.BlockSpec((seq, RWKV_WIDTH), lambda b: (b, 0)),
            pl.BlockSpec((1,) + st_shape[1:], lambda b: (b, 0, 0, 0, 0)),
        ),
        scratch_shapes=[pltpu.VMEM((seq, RWKV_WIDTH), F32)]
        + [pltpu.VMEM((2, seq, RWKV_WIDTH), F32)] * 4
        + [pltpu.VMEM((2, RWKV_HEADS, RWKV_HEAD_DIM, RWKV_HEAD_DIM), F32)],
        compiler_params=_params(1),
        name="rwkv",
    )(p, p, p, p, vec, w2, a2, g2, init)


def _group_rmsnorm(x, bd, width, gain):
    ms = _dot01(x * x, bd) * (1.0 / width)
    return x * lax.rsqrt(ms + NORM_EPS) * gain


def _rope(x, tab_ref):
    half = 8
    x_next = pltpu.roll(x, ATT_WIDTH - half, 1)
    x_prev = pltpu.roll(x, half, 1)
    return x * tab_ref[0] + x_next * tab_ref[1] + x_prev * tab_ref[2]


def _attn_kernel(*refs, layer, latent):
    if latent:
        (q_ref, k_ref, v_ref, gv_ref, lam_ref, tab_ref, ck_ref, cv_ref, y_ref, ka_s, va_s, qn_s) = refs
    else:
        (q_ref, k_ref, v_ref, gv_ref, lam_ref, y_ref, kn_ref, vo_ref, ka_s, va_s, qn_s) = refs
    seq = q_ref.shape[0]
    bd32 = _block_diag_ones(ATT_WIDTH, ATT_QK_DIM)
    qn = _group_rmsnorm(q_ref[...], bd32, ATT_QK_DIM, gv_ref[0:1, :])
    kn = _group_rmsnorm(k_ref[...], bd32, ATT_QK_DIM, gv_ref[1:2, :])
    v = v_ref[...]
    if latent:
        past = ck_ref.shape[2]
        qn = _rope(qn, tab_ref)
        kn = _rope(kn, tab_ref)
        ka_s[0:past, :] = ck_ref[0, 0].astype(BF16)
        va_s[0:past, :] = cv_ref[0, 0].astype(BF16)
    else:
        past = 0
        kn_ref[...] = kn
        vo_ref[...] = v
    ka_s[past:past + seq, :] = kn.astype(BF16)
    va_s[past:past + seq, :] = v.astype(BF16)
    qn_s[...] = qn

    lam_init = 0.8 - 0.6 * math.exp(-0.3 * layer)
    lv = lam_ref[...]
    lam = (jnp.exp(jnp.sum(lv[0:1, :] * lv[1:2, :], axis=-1, keepdims=True))
           - jnp.exp(jnp.sum(lv[2:3, :] * lv[3:4, :], axis=-1, keepdims=True)) + lam_init)
    scale = ATT_QK_DIM ** -0.5
    bd64 = _block_diag_ones(LANES, ATT_V_DIM)
    lane = lax.broadcasted_iota(jnp.int32, (1, LANES), 1)
    tq = min(ATT_Q_BLOCK, seq)

    def q_block(i, carry):
        rows = pl.ds(pl.multiple_of(i * tq, tq), tq)
        outs = []
        for pair in range(ATT_HEADS // 2):
            cols = slice(pair * LANES, (pair + 1) * LANES)
            q_pair = qn_s[rows, cols]
            k_pair = ka_s[:, cols]
            v_pair = va_s[:, cols]
            o_heads = []
            for hh in range(2):
                probs = []
                for m in range(2):
                    lo = hh * 2 * ATT_QK_DIM + m * ATT_QK_DIM
                    sel = (lane >= lo) & (lane < lo + ATT_QK_DIM)
                    qm = jnp.where(sel, q_pair, 0.0).astype(BF16)
                    sc = _dot_nt(qm, k_pair) * scale
                    sc = sc - jnp.max(sc, axis=-1, keepdims=True)
                    e = jnp.exp(sc)
                    probs.append(e / jnp.sum(e, axis=-1, keepdims=True))
                wgt = probs[0] - lam * probs[1]
                o_heads.append(_dot(wgt.astype(BF16), v_pair))
            o = jnp.where(lane < ATT_V_DIM, o_heads[0], o_heads[1])
            ms = _dot01(o * o, bd64) * (1.0 / ATT_V_DIM)
            outs.append(o * lax.rsqrt(ms + NORM_EPS) * gv_ref[2:3, cols] * (1.0 - lam_init))
        y_ref[rows, :] = jnp.concatenate(outs, axis=1)
        return carry

    lax.fori_loop(0, seq // tq, q_block, 0)


def _attn_branch(p, seq, n_batch, layer, gv, lam, rope_tab=None, cache_k=None, cache_v=None):
    latent = cache_k is not None
    blk = lambda col: pl.BlockSpec((seq, ATT_WIDTH), lambda b: (b, col))
    in_specs = [blk(P_QA), blk(P_KA), blk(P_VA), _const_spec((SUBLANES, ATT_WIDTH)), _const_spec((4, ATT_QK_DIM))]
    args = [p, p, p, gv, lam]
    seq_out = jax.ShapeDtypeStruct((n_batch * seq, ATT_WIDTH), F32)
    out_blk = pl.BlockSpec((seq, ATT_WIDTH), lambda b: (b, 0))
    if latent:
        past = cache_k.shape[2]
        in_specs += [
            _const_spec((3, seq, ATT_WIDTH)),
            pl.BlockSpec((1, 1, past, ATT_WIDTH), lambda b: (b, layer, 0, 0)),
            pl.BlockSpec((1, 1, past, ATT_WIDTH), lambda b: (b, layer, 0, 0)),
        ]
        args += [rope_tab, cache_k, cache_v]
        out_shape, out_specs = seq_out, out_blk
    else:
        past = 0
        out_shape, out_specs = (seq_out,) * 3, (out_blk,) * 3
    return pl.pallas_call(
        functools.partial(_attn_kernel, layer=layer, latent=latent),
        out_shape=out_shape,
        grid=(n_batch,),
        in_specs=in_specs,
        out_specs=out_specs,
        scratch_shapes=[
            pltpu.VMEM((past + seq, ATT_WIDTH), BF16),
            pltpu.VMEM((past + seq, ATT_WIDTH), BF16),
            pltpu.VMEM((seq, ATT_WIDTH), F32),
        ],
        compiler_params=_params(1),
        name="attn",
    )(*args)


def _route(logits):
    lane = lax.broadcasted_iota(jnp.int32, logits.shape, 1)
    big = jnp.int32(1 << 20)
    neg = -jnp.inf
    gmask = lane < N_GROUPS
    gl = jnp.where(gmask, logits, neg)
    gmax = jnp.max(gl, axis=-1, keepdims=True)
    gsum = jnp.sum(jnp.where(gmask, jnp.exp(gl - gmax), 0.0), axis=-1, keepdims=True)
    g_val = 1.0 / gsum
    g_idx = jnp.min(jnp.where(gmask & (gl == gmax), lane, big), axis=-1, keepdims=True)
    e_lane = lane - N_GROUPS
    sel = (e_lane >= 0) & (e_lane < N_EXPERTS) & ((e_lane // EXPERTS_PER_GROUP) == g_idx)
    l1 = jnp.max(jnp.where(sel, logits, neg), axis=-1, keepdims=True)
    i1 = jnp.min(jnp.where(sel & (logits == l1), lane, big), axis=-1, keepdims=True)
    sel2 = sel & (lane != i1)
    l2 = jnp.max(jnp.where(sel2, logits, neg), axis=-1, keepdims=True)
    i2 = jnp.min(jnp.where(sel2 & (logits == l2), lane, big), axis=-1, keepdims=True)
    e2 = jnp.exp(l2 - l1)
    w1 = 1.0 / (1.0 + e2)
    w2 = e2 * w1
    return jnp.where(lane == i1, g_val * w1, 0.0) + jnp.where(lane == i2, g_val * w2, 0.0)


def _post_kernel(x_ref, ys_ref, yr_ref, ya_ref, mod_ref, g_ref, wo_ref, wr_ref, br_ref, w1_ref, w3_ref, w2_ref, o_ref):
    mix = _dot(ys_ref[...].astype(BF16), wo_ref[0:SSD_WIDTH, :])
    mix = mix + _dot(yr_ref[...].astype(BF16), wo_ref[SSD_WIDTH:SSD_WIDTH + RWKV_WIDTH, :])
    mix = mix + _dot(ya_ref[...].astype(BF16), wo_ref[SSD_WIDTH + RWKV_WIDTH:, :])
    x = x_ref[...] + mod_ref[0, 2:3, :] * mix
    h = _modulated_norm(x, g_ref[...], mod_ref[0, 3:4, :], mod_ref[0, 4:5, :])
    gate = _route(_dot(h, wr_ref[...], HIGHEST) + br_ref[...])
    hb = h.astype(BF16)
    acc = jnp.zeros(x.shape, F32)
    for e in range(N_EXPERTS):
        a = _dot(hb, w1_ref[e])
        b = _dot(hb, w3_ref[e])
        hid = _silu(a) * b * gate[:, N_GROUPS + e:N_GROUPS + e + 1]
        acc = acc + _dot(hid.astype(BF16), w2_ref[e])
    o_ref[...] = x + mod_ref[0, 5:6, :] * acc


def _post(x, y_ssd, y_rwkv, y_att, mod, gain, wo, wr, br, w1, w3, w2, mod_row):
    n_tok = x.shape[0]
    tok = lambda width: pl.BlockSpec((TOKEN_TILE, width), lambda i: (i, 0))
    return pl.pallas_call(
        _post_kernel,
        out_shape=jax.ShapeDtypeStruct((n_tok, D_MODEL), F32),
        grid=(n_tok // TOKEN_TILE,),
        in_specs=[
            tok(D_MODEL), tok(SSD_WIDTH), tok(RWKV_WIDTH), tok(ATT_WIDTH),
            pl.BlockSpec((1, 6, D_MODEL), lambda i: (mod_row(i), 0, 0)),
            _const_spec((1, D_MODEL)),
            _const_spec((D_MODEL, D_MODEL)),
            _const_spec((D_MODEL, LANES)),
            _const_spec((1, LANES)),
            _const_spec((N_EXPERTS, D_MODEL, EXPERT_HIDDEN)),
            _const_spec((N_EXPERTS, D_MODEL, EXPERT_HIDDEN)),
            _const_spec((N_EXPERTS, EXPERT_HIDDEN, D_MODEL)),
        ],
        out_specs=tok(D_MODEL),
        compiler_params=_params(1),
        name="post",
    )(x, y_ssd, y_rwkv, y_att, mod, gain, wo, wr, br, w1, w3, w2)


def _pad_rows(a, rows):
    return jnp.pad(a, ((0, rows - a.shape[0]), (0, 0)))


def _reorder_w_in(w_in):
    widths = (640, 384, 12, 384, 384, 384, 64, 64, 128, 256, 256, 256)
    offs = [0]
    for w in widths:
        offs.append(offs[-1] + w)
    g = [w_in[..., offs[i]:offs[i + 1]] for i in range(len(widths))]
    xbc, z, dt, r, k, v, xw, xa, xg, qa, ka, va = g
    pad = jnp.zeros(w_in.shape[:-1] + (LANES - 12,), w_in.dtype)
    return jnp.concatenate([z, r, k, v, xw, xa, xg, qa, ka, va, xbc, dt, pad], axis=-1)


def _rope_tables(n_tokens):
    half = ATT_QK_DIM // 2
    quarter = half // 2
    tok = jnp.arange(n_tokens)
    row = (tok // GRID_W).astype(F32)
    col = (tok % GRID_W).astype(F32)
    inv = ROPE_BASE ** (-jnp.arange(quarter, dtype=F32) / quarter)
    lane = jnp.arange(ATT_WIDTH)
    within = lane % ATT_QK_DIM
    use_col = within >= half
    freq = inv[within % quarter]
    second = (within % half) >= quarter
    pos = jnp.where(use_col[None, :], col[:, None], row[:, None])
    ang = pos * freq[None, :]
    cos = jnp.cos(ang)
    sin = jnp.sin(ang)
    return jnp.stack([cos, jnp.where(second[None, :], 0.0, -sin), jnp.where(second[None, :], sin, 0.0)])


def kernel(x_prompt, x_sample, c, cache_attn_k, cache_attn_v, state_ssd, state_rwkv, c_ctx, ada_w, ada_b, norm1_g, norm2_g, w_in, ssd_conv_w, ssd_conv_b, ssd_dt_bias, ssd_a_log, ssd_d, ssd_norm_g, rwkv_w0, rwkv_w2, rwkv_a0, rwkv_a2, rwkv_g2, rwkv_kk, rwkv_ka, rwkv_rk, rwkv_ln_g, rwkv_ln_b, att_qnorm_g, att_knorm_g, att_lambda, att_onorm_g, w_out, moe_wg, moe_bg, moe_we, moe_be, moe_w1, moe_w3, moe_w2):
    n_ctx, s_ctx, _ = x_prompt.shape
    n_lat, s_lat, _ = x_sample.shape
    past = cache_attn_k.shape[2]

    cond = _pad_rows(jnp.concatenate([c_ctx[None, :], c], axis=0), MOD_ROWS)
    mod = _ada_mod(cond, ada_w, ada_b).reshape(DEPTH, MOD_ROWS, 6, D_MODEL)

    w_in_p = _reorder_w_in(w_in).astype(BF16)
    w_out_b = w_out.astype(BF16)
    w1_b, w3_b, w2_b = moe_w1.astype(BF16), moe_w3.astype(BF16), moe_w2.astype(BF16)
    w_route = jnp.concatenate(
        [moe_wg, moe_we, jnp.zeros((DEPTH, D_MODEL, LANES - N_GROUPS - N_EXPERTS), F32)], axis=-1)
    b_route = jnp.concatenate(
        [moe_bg, moe_be, jnp.zeros((DEPTH, LANES - N_GROUPS - N_EXPERTS), F32)], axis=-1)[:, None, :]
    rope_tab = _rope_tables(s_lat)
    cache_k = cache_attn_k.reshape(n_lat, DEPTH, past, ATT_WIDTH)
    cache_v = cache_attn_v.reshape(n_lat, DEPTH, past, ATT_WIDTH)
    zero_ssd = jnp.zeros((n_ctx, 2, SSD_HEADS, SSD_HEAD_DIM, SSD_STATE), F32)
    zero_rwkv = jnp.zeros((n_ctx, 2, RWKV_HEADS, RWKV_HEAD_DIM, RWKV_HEAD_DIM), F32)

    ctx_row = lambda i: 0
    lat_row = lambda i: 1 + i // (s_lat // TOKEN_TILE)

    xp = x_prompt.reshape(n_ctx * s_ctx, D_MODEL)
    xs = x_sample.reshape(n_lat * s_lat, D_MODEL)
    new_k, new_v, new_ssd, new_rwkv = [], [], [], []
    for l in range(DEPTH):
        conv = _pad_rows(jnp.concatenate([ssd_conv_w[l], ssd_conv_b[l][None, :]], axis=0), SUBLANES)
        ssd_vec = _pad_rows(jnp.stack([jnp.repeat(ssd_d[l], SSD_HEAD_DIM), ssd_norm_g[l]]), SUBLANES)
        dtp = _pad_rows(jnp.pad(jnp.stack([ssd_dt_bias[l].reshape(-1), ssd_a_log[l].reshape(-1)]),
                                ((0, 0), (0, LANES - 2 * SSD_HEADS))), SUBLANES)
        rwkv_vec = _pad_rows(jnp.stack([
            rwkv_kk[l], rwkv_ka[l], rwkv_rk[l].reshape(-1), rwkv_ln_g[l], rwkv_ln_b[l],
            rwkv_w0[l, 0], rwkv_w0[l, 1], rwkv_a0[l, 0], rwkv_a0[l, 1]]), 16)
        tile32 = lambda g: jnp.tile(g, ATT_WIDTH // ATT_QK_DIM)
        att_vec = _pad_rows(jnp.stack([tile32(att_qnorm_g[l]), tile32(att_knorm_g[l]),
                                       jnp.tile(att_onorm_g[l], ATT_WIDTH // ATT_V_DIM)]), SUBLANES)
        g1 = norm1_g[l][None, :]
        g2 = norm2_g[l][None, :]
        post_w = (mod[l], g2, w_out_b[l], w_route[l], b_route[l], w1_b[l], w3_b[l], w2_b[l])

        p = _in_proj(xp, mod[l], g1, w_in_p[l], ctx_row)
        y_ssd, st_ssd = _ssd_branch(p, s_ctx, n_ctx, conv, ssd_vec, dtp, zero_ssd)
        y_rwkv, st_rwkv = _rwkv_branch(p, s_ctx, n_ctx, rwkv_vec, rwkv_w2[l], rwkv_a2[l], rwkv_g2[l], zero_rwkv)
        y_att, k_att, v_att = _attn_branch(p, s_ctx, n_ctx, l, att_vec, att_lambda[l])
        xp = _post(xp, y_ssd, y_rwkv, y_att, *post_w, ctx_row)
        new_k.append(k_att.reshape(n_ctx, s_ctx, ATT_HEADS, 2, ATT_QK_DIM))
        new_v.append(v_att.reshape(n_ctx, s_ctx, ATT_HEADS, ATT_V_DIM))
        new_ssd.append(st_ssd)
        new_rwkv.append(st_rwkv)

        p = _in_proj(xs, mod[l], g1, w_in_p[l], lat_row)
        y_ssd, _ = _ssd_branch(p, s_lat, n_lat, conv, ssd_vec, dtp, state_ssd[:, l])
        y_rwkv, _ = _rwkv_branch(p, s_lat, n_lat, rwkv_vec, rwkv_w2[l], rwkv_a2[l], rwkv_g2[l], state_rwkv[:, l])
        y_att = _attn_branch(p, s_lat, n_lat, l, att_vec, att_lambda[l], rope_tab, cache_k, cache_v)
        xs = _post(xs, y_ssd, y_rwkv, y_att, *post_w, lat_row)

    return (xp.reshape(n_ctx, s_ctx, D_MODEL), xs.reshape(n_lat, s_lat, D_MODEL),
            jnp.stack(new_k, axis=1), jnp.stack(new_v, axis=1),
            jnp.stack(new_ssd, axis=1), jnp.stack(new_rwkv, axis=1))
```

```python
import functools
import math

import jax
import jax.numpy as jnp
from jax import lax
from jax.experimental import pallas as pl
from jax.experimental.pallas import tpu as pltpu

F32 = jnp.float32
BF16 = jnp.bfloat16
HIGHEST = lax.Precision.HIGHEST

D_MODEL = 1024
DEPTH = 2
GRID_W = 64
SSD_WIDTH = 384
SSD_HEADS = 6
SSD_GROUPS = 2
SSD_STATE = 64
SSD_HEAD_DIM = 64
SSD_CONV = 5
SSD_CONV_DIM = 640
SSD_CHUNK = 128
RWKV_WIDTH = 384
RWKV_HEADS = 6
RWKV_HEAD_DIM = 64
RWKV_CHUNK = 64
RWKV_SUB = 8
RWKV_PAIRS = RWKV_HEADS // 2
RWKV_GN_EPS = 64e-5
ATT_WIDTH = 256
ATT_HEADS = 4
ATT_QK_DIM = 32
ATT_V_DIM = 64
ATT_Q_BLOCK = 256
ROPE_BASE = 10000.0
N_GROUPS = 4
EXPERTS_PER_GROUP = 4
N_EXPERTS = 16
EXPERT_HIDDEN = 256
NORM_EPS = 1e-6

LANES = 128
SUBLANES = 8
TOKEN_TILE = 512
VMEM_LIMIT_BYTES = 56 * 1024 * 1024

P_WIDTH = 3328
P_Z, P_R, P_K, P_V = 0, 1, 2, 3
P_LORA, P_QA, P_KA, P_VA = 6, 7, 8, 9
P_XBC = 4
P_DT = 25
MOD_ROWS = 16


def _dot(a, b, precision=None):
    return lax.dot_general(a, b, (((1,), (0,)), ((), ())), precision=precision, preferred_element_type=F32)


def _dot_nt(a, b, precision=None):
    return lax.dot_general(a, b, (((1,), (1,)), ((), ())), precision=precision, preferred_element_type=F32)


def _dot_tn(a, b, precision=None):
    return lax.dot_general(a, b, (((0,), (0,)), ((), ())), precision=precision, preferred_element_type=F32)


def _bdot(a, b):
    return _dot(a.astype(BF16), b.astype(BF16))


def _bdot_nt(a, b):
    return _dot_nt(a.astype(BF16), b.astype(BF16))


def _bdot_tn(a, b):
    return _dot_tn(a.astype(BF16), b.astype(BF16))


def _bf16_terms(a, terms):
    parts = []
    for _ in range(terms):
        part = a.astype(BF16)
        a = a - part.astype(F32)
        parts.append(part)
    return parts


def _dot01(a, b01, terms=3):
    b = b01.astype(BF16)
    return sum(_dot(part, b) for part in _bf16_terms(a, terms))


def _dot01_left(a01, b, terms=3):
    a = a01.astype(BF16)
    return sum(_dot(a, part) for part in _bf16_terms(b, terms))


def _sigmoid(x):
    return 0.5 * jnp.tanh(0.5 * x) + 0.5


def _silu(x):
    return x * _sigmoid(x)


def _softplus(x):
    return jnp.maximum(x, 0.0) + jnp.log1p(jnp.exp(-jnp.abs(x)))


def _block_diag_ones(n, blk):
    r = lax.broadcasted_iota(jnp.int32, (n, n), 0) // blk
    c = lax.broadcasted_iota(jnp.int32, (n, n), 1) // blk
    return (r == c).astype(F32)


def _const_spec(shape):
    zeros = (0,) * len(shape)
    return pl.BlockSpec(shape, lambda *_: zeros, pipeline_mode=pl.Buffered(1))


def _params(n_axes=1):
    return pltpu.CompilerParams(dimension_semantics=("arbitrary",) * n_axes, vmem_limit_bytes=VMEM_LIMIT_BYTES)


def _ada_kernel(c_ref, w_ref, b_ref, o_ref):
    o_ref[0] = _dot(_silu(c_ref[...]), w_ref[0], HIGHEST) + b_ref[0]


def _ada_mod(cond, ada_w, ada_b):
    tn = 512
    n = 6 * D_MODEL
    return pl.pallas_call(
        _ada_kernel,
        out_shape=jax.ShapeDtypeStruct((DEPTH, MOD_ROWS, n), F32),
        grid=(DEPTH, n // tn),
        in_specs=[
            pl.BlockSpec((MOD_ROWS, D_MODEL), lambda l, j: (0, 0)),
            pl.BlockSpec((1, D_MODEL, tn), lambda l, j: (l, 0, j)),
            pl.BlockSpec((1, 1, tn), lambda l, j: (l, 0, j)),
        ],
        out_specs=pl.BlockSpec((1, MOD_ROWS, tn), lambda l, j: (l, 0, j)),
        compiler_params=_params(2),
        name="ada_mod",
    )(cond, ada_w, ada_b.reshape(DEPTH, 1, n))


def _modulated_norm(x, gain, shift, scale):
    y = x * lax.rsqrt(jnp.mean(x * x, axis=-1, keepdims=True) + NORM_EPS)
    return y * gain * (1.0 + scale) + shift


def _in_proj_kernel(x_ref, mod_ref, g_ref, w_ref, o_ref):
    h = _modulated_norm(x_ref[...], g_ref[...], mod_ref[0, 0:1, :], mod_ref[0, 1:2, :]).astype(BF16)
    cw = 512
    for c0 in range(0, P_WIDTH, cw):
        c1 = min(c0 + cw, P_WIDTH)
        o_ref[:, c0:c1] = _dot(h, w_ref[:, c0:c1])


def _in_proj(x, mod, gain, w, mod_row):
    n_tok = x.shape[0]
    return pl.pallas_call(
        _in_proj_kernel,
        out_shape=jax.ShapeDtypeStruct((n_tok, P_WIDTH), F32),
        grid=(n_tok // TOKEN_TILE,),
        in_specs=[
            pl.BlockSpec((TOKEN_TILE, D_MODEL), lambda i: (i, 0)),
            pl.BlockSpec((1, 6, D_MODEL), lambda i: (mod_row(i), 0, 0)),
            _const_spec((1, D_MODEL)),
            _const_spec((D_MODEL, P_WIDTH)),
        ],
        out_specs=pl.BlockSpec((TOKEN_TILE, P_WIDTH), lambda i: (i, 0)),
        compiler_params=_params(1),
        name="in_proj",
    )(x, mod, gain, w)


def _ssd_kernel(xbc_ref, z_ref, dt_ref, cw_ref, vec_ref, dtp_ref, init_ref, y_ref, fin_ref,
                pad_ref, xc_ref, dt_s, da_s, y_s, st_ref):
    seq = xbc_ref.shape[0]
    q = SSD_CHUNK
    n_chunks = seq // q
    rep = SSD_HEADS // SSD_GROUPS
    pad_ref[0:SUBLANES, :] = jnp.zeros((SUBLANES, SSD_CONV_DIM), F32)
    pad_ref[seq + SUBLANES:seq + 2 * SUBLANES, :] = jnp.zeros((SUBLANES, SSD_CONV_DIM), F32)
    pad_ref[SUBLANES:seq + SUBLANES, :] = xbc_ref[...]
    for r0 in range(0, seq, q):
        acc = jnp.zeros((q, SSD_CONV_DIM), F32) + cw_ref[SSD_CONV:SSD_CONV + 1, :]
        for j in range(SSD_CONV):
            off = r0 + SUBLANES + j - SSD_CONV // 2
            acc = acc + cw_ref[j:j + 1, :] * pad_ref[off:off + q, :]
        xc_ref[r0:r0 + q, :] = _silu(acc)
    dt = _softplus(dt_ref[...] + dtp_ref[0:1, :])
    dt_s[...] = dt
    da_s[...] = dt * (-jnp.exp(dtp_ref[1:2, :]))

    row = lax.broadcasted_iota(jnp.int32, (q, q), 0)
    col = lax.broadcasted_iota(jnp.int32, (q, q), 1)
    keep = [col <= row, col >= row]
    tri = [k.astype(F32) for k in keep]
    wide = SSD_HEADS * SSD_HEAD_DIM
    src_c = lax.broadcasted_iota(jnp.int32, (LANES, wide), 0)
    head_c = lax.broadcasted_iota(jnp.int32, (LANES, wide), 1) // SSD_HEAD_DIM
    to_chan = [(src_c == d * SSD_HEADS + head_c).astype(F32) for d in range(2)]
    grp_ok = (src_c // SSD_STATE) == (head_c // rep)
    lane = lax.broadcasted_iota(jnp.int32, (q, LANES), 1)
    zero_pn = jnp.zeros((SSD_HEAD_DIM, SSD_STATE), F32)
    for d in range(2):
        rows_pn = []
        for h in range(SSD_HEADS):
            s = init_ref[0, 0, d, h]
            halves = [s, zero_pn] if h // rep == 0 else [zero_pn, s]
            rows_pn.append(jnp.concatenate(halves, axis=1))
        st_ref[d] = jnp.concatenate(rows_pn, axis=0).T

    def chunk_body(i, carry):
        for d in range(2):
            c = i if d == 0 else n_chunks - 1 - i
            rows = pl.ds(pl.multiple_of(c * q, q), q)
            last = q - 1 if d == 0 else 0
            da = da_s[rows, :]
            g = _dot01_left(tri[d], da)
            g_t = g.T
            chan = _dot01(jnp.concatenate([g, dt_s[rows, :]], axis=0), to_chan[d])
            g_chan = chan[0:q]
            dt_chan = chan[q:]
            tot_chan = g_chan[last:last + 1, :]
            xcc = xc_ref[rows, :]
            x_all = xcc[:, 0:wide] * dt_chan
            x_dec = x_all * jnp.exp(tot_chan - g_chan)
            bw = xcc[:, wide:wide + LANES].astype(BF16)
            cw = xcc[:, wide + LANES:wide + 2 * LANES]
            cwb = cw.astype(BF16)
            cb = [_dot_nt(jnp.where(lane < SSD_STATE, cw, 0.0).astype(BF16), bw),
                  _dot_nt(jnp.where(lane < SSD_STATE, 0.0, cw).astype(BF16), bw)]
            tiles = []
            for h in range(SSD_HEADS):
                cidx = d * SSD_HEADS + h
                diff = g[:, cidx:cidx + 1] - g_t[cidx:cidx + 1, :]
                lm = jnp.exp(jnp.where(keep[d], diff, -jnp.inf))
                tiles.append((cb[h // rep] * lm).astype(BF16))
            y_diag = []
            for pr in range(SSD_HEADS // 2):
                x_pair = x_all[:, pr * LANES:(pr + 1) * LANES]
                x_bd = jnp.concatenate([jnp.where(lane < SSD_HEAD_DIM, x_pair, 0.0),
                                        jnp.where(lane < SSD_HEAD_DIM, 0.0, x_pair)], axis=0).astype(BF16)
                y_diag.append(_dot(jnp.concatenate(tiles[2 * pr:2 * pr + 2], axis=1), x_bd))
            st = st_ref[d]
            y_s[d, rows, :] = jnp.concatenate(y_diag, axis=1) + _dot(cwb, st.astype(BF16)) * jnp.exp(g_chan)
            st_new = jnp.exp(tot_chan) * st + _dot_tn(bw, x_dec.astype(BF16))
            st_ref[d] = jnp.where(grp_ok, st_new, 0.0)
        return carry

    lax.fori_loop(0, n_chunks, chunk_body, 0)
    for d in range(2):
        st = st_ref[d].T
        for h in range(SSD_HEADS):
            grp = h // rep
            fin_ref[0, d, h] = st[h * SSD_HEAD_DIM:(h + 1) * SSD_HEAD_DIM, grp * SSD_STATE:(grp + 1) * SSD_STATE]

    y = y_s[0] + y_s[1] + vec_ref[0:1, :] * xc_ref[:, 0:SSD_WIDTH]
    y = y * _silu(z_ref[...])
    y = y * lax.rsqrt(jnp.mean(y * y, axis=-1, keepdims=True) + NORM_EPS)
    y_ref[...] = y * vec_ref[1:2, :]


def _ssd_branch(p, seq, n_batch, conv, vec, dtp, init, init_layer):
    st_shape = (n_batch, 2, SSD_HEADS, SSD_HEAD_DIM, SSD_STATE)
    return pl.pallas_call(
        _ssd_kernel,
        out_shape=(jax.ShapeDtypeStruct((n_batch * seq, SSD_WIDTH), F32), jax.ShapeDtypeStruct(st_shape, F32)),
        grid=(n_batch,),
        in_specs=[
            pl.BlockSpec((seq, SSD_CONV_DIM), lambda b: (b, P_XBC)),
            pl.BlockSpec((seq, SSD_WIDTH), lambda b: (b, P_Z)),
            pl.BlockSpec((seq, LANES), lambda b: (b, P_DT)),
            _const_spec((SUBLANES, SSD_CONV_DIM)),
            _const_spec((SUBLANES, SSD_WIDTH)),
            _const_spec((SUBLANES, LANES)),
            pl.BlockSpec((1, 1) + st_shape[1:], lambda b: (b, init_layer, 0, 0, 0, 0)),
        ],
        out_specs=(
            pl.BlockSpec((seq, SSD_WIDTH), lambda b: (b, 0)),
            pl.BlockSpec((1,) + st_shape[1:], lambda b: (b, 0, 0, 0, 0)),
        ),
        scratch_shapes=[
            pltpu.VMEM((seq + 2 * SUBLANES, SSD_CONV_DIM), F32),
            pltpu.VMEM((seq, SSD_CONV_DIM), F32),
            pltpu.VMEM((seq, LANES), F32),
            pltpu.VMEM((seq, LANES), F32),
            pltpu.VMEM((2, seq, SSD_WIDTH), F32),
            pltpu.VMEM((2, 2 * SSD_STATE, SSD_HEADS * SSD_HEAD_DIM), F32),
        ],
        compiler_params=_params(1),
        name="ssd",
    )(p, p, p, conv, vec, dtp, init)


def _rwkv_kernel(r_ref, k_ref, v_ref, lora_ref, vec_ref, w2_ref, a2_ref, g2_ref, init_ref, y_ref, fin_ref,
                 kkn_s, lw_s, kd_s, kka_s, y_s, st_ref):
    seq = r_ref.shape[0]
    t = RWKV_CHUNK
    n_chunks = seq // t
    hd = RWKV_HEAD_DIM
    n2 = 2 * t
    n4 = 4 * t
    r = r_ref[...]
    k = k_ref[...]
    lora = lora_ref[...]
    xw = lora[:, 0:64]
    xa = lora[:, 64:128]
    xg = lora[:, 128:256]
    bd = _block_diag_ones(RWKV_WIDTH, hd)
    kkp = k * vec_ref[0:1, :]
    kss = _dot01(kkp * kkp, bd)
    kkn = kkp * lax.rsqrt(jnp.maximum(kss, 1e-24))
    kkn_s[...] = kkn
    tw = jnp.tanh(xw)
    zero_hh = jnp.zeros((hd, hd), F32)
    for d in range(2):
        wl = vec_ref[5 + d:6 + d, :] + _bdot(tw, w2_ref[d])
        lw_s[d] = -math.exp(-0.5) * _sigmoid(wl)
        a = _sigmoid(vec_ref[7 + d:8 + d, :] + _bdot(xa, a2_ref[d]))
        kd_s[d] = k * (1.0 + (a - 1.0) * vec_ref[1:2, :])
        kka_s[d] = kkn * a
        for p in range(RWKV_PAIRS):
            st_ref[d, p] = jnp.concatenate([
                jnp.concatenate([init_ref[0, 0, d, 2 * p], zero_hh], axis=1),
                jnp.concatenate([zero_hh, init_ref[0, 0, d, 2 * p + 1]], axis=1)], axis=0)

    ri = lax.broadcasted_iota(jnp.int32, (n4, n4), 0)
    ci = lax.broadcasted_iota(jnp.int32, (n4, n4), 1)
    rt = ri % t
    ct = ci % t
    top = ri < n2
    keep = [(top & (ct < rt)) | (~top & (ct <= rt)), (top & (ct > rt)) | (~top & (ct >= rt))]
    si = lax.broadcasted_iota(jnp.int32, (n2, n2), 0) // RWKV_SUB
    sj = lax.broadcasted_iota(jnp.int32, (n2, n2), 1) // RWKV_SUB
    sub_diag = si == sj
    tr = lax.broadcasted_iota(jnp.int32, (t, t), 0)
    tc = lax.broadcasted_iota(jnp.int32, (t, t), 1)
    tri = [(tc <= tr).astype(F32), (tc >= tr).astype(F32)]
    lo = lax.broadcasted_iota(jnp.int32, (t, LANES), 1) < hd

    def block_diag(x):
        return jnp.concatenate([jnp.where(lo, x, 0.0), jnp.where(lo, 0.0, x)], axis=0)

    def chunk_body(i, carry):
        ch = []
        for d in range(2):
            c = i if d == 0 else n_chunks - 1 - i
            rows = pl.ds(pl.multiple_of(c * t, t), t)
            lw = lw_s[d, rows, :]
            g = _dot01_left(tri[d], lw)
            e_g = jnp.exp(g)
            e_ng = jnp.exp(-g)
            kap = kkn_s[rows, :] * jnp.exp(g - lw)
            rr = r_ref[rows, :] * e_g
            bb = kka_s[d, rows, :] * e_ng
            kk = kd_s[d, rows, :] * e_ng
            vv = v_ref[rows, :]
            e_tot = jnp.exp(jnp.sum(lw, axis=0, keepdims=True))
            for p in range(RWKV_PAIRS):
                sl = slice(p * LANES, (p + 1) * LANES)
                ch.append(dict(
                    d=d, p=p, rows=rows,
                    lq=jnp.concatenate([block_diag(kap[:, sl]), block_diag(rr[:, sl])], axis=0).astype(BF16),
                    rq=jnp.concatenate([block_diag(bb[:, sl]), block_diag(kk[:, sl])], axis=0).astype(BF16),
                    vb=block_diag(vv[:, sl]),
                    s0=st_ref[d, p], etot=e_tot[:, sl]))
        for x in ch:
            x["mm"] = jnp.where(keep[x["d"]], _dot_nt(x["lq"], x["rq"]), 0.0)
            x["c0"] = _bdot_nt(x["lq"], x["s0"])
        for x in ch:
            x["base"] = x["c0"] + _bdot(x["mm"][:, n2:], x["vb"])
            ab = x["mm"][0:n2, 0:n2]
            x["nd"] = jnp.where(sub_diag, ab, 0.0)
            x["rhs"] = jnp.concatenate([jnp.where(sub_diag, 0.0, ab), x["base"][0:n2]], axis=1)
        for x in ch:
            x["nd2"] = _bdot(x["nd"], x["nd"])
            x["y"] = x["rhs"] - _bdot(x["nd"], x["rhs"])
        for x in ch:
            x["nd4"] = _bdot(x["nd2"], x["nd2"])
            x["y"] = x["y"] + _bdot(x["nd2"], x["y"])
        for x in ch:
            x["y"] = x["y"] + _bdot(x["nd4"], x["y"])
            x["m"] = x["y"][:, 0:n2]
            x["x"] = x["y"][:, n2:]
        for x in ch:
            x["m2"] = _bdot(x["m"], x["m"])
            x["x"] = x["x"] - _bdot(x["m"], x["x"])
        for x in ch:
            x["m4"] = _bdot(x["m2"], x["m2"])
            x["x"] = x["x"] + _bdot(x["m2"], x["x"])
        for x in ch:
            x["x"] = x["x"] + _bdot(x["m4"], x["x"])
            x["zu"] = -x["x"]
        for x in ch:
            yb = x["base"][n2:] + _bdot(x["mm"][n2:, 0:n2], x["zu"])
            x["yo"] = yb[0:t] + yb[t:]
            uv = jnp.concatenate([x["zu"], x["vb"]], axis=0)
            st_ref[x["d"], x["p"]] = (x["s0"] + _bdot_tn(uv, x["rq"])) * x["etot"]
        for d in range(2):
            rows = ch[d * RWKV_PAIRS]["rows"]
            y_s[d, rows, :] = jnp.concatenate([x["yo"] for x in ch if x["d"] == d], axis=1)
        return carry

    lax.fori_loop(0, n_chunks, chunk_body, 0)
    for d in range(2):
        for p in range(RWKV_PAIRS):
            s = st_ref[d, p]
            fin_ref[0, d, 2 * p] = s[0:hd, 0:hd]
            fin_ref[0, d, 2 * p + 1] = s[hd:, hd:]

    yf = y_s[0] + y_s[1]
    inv = 1.0 / hd
    mu = _dot01(yf, bd) * inv
    yc = yf - mu
    var = _dot01(yc * yc, bd) * inv
    yn = yc * lax.rsqrt(var + RWKV_GN_EPS) * vec_ref[3:4, :] + vec_ref[4:5, :]
    v = v_ref[...]
    bonus = _dot01(r * k * vec_ref[2:3, :], bd) * v
    gate = _bdot(_sigmoid(xg), g2_ref[...])
    y_ref[...] = (yn + bonus) * gate


def _rwkv_branch(p, seq, n_batch, vec, w2, a2, g2, init, init_layer):
    st_shape = (n_batch, 2, RWKV_HEADS, RWKV_HEAD_DIM, RWKV_HEAD_DIM)
    wide = lambda col: pl.BlockSpec((seq, RWKV_WIDTH), lambda b: (b, col))
    return pl.pallas_call(
        _rwkv_kernel,
        out_shape=(jax.ShapeDtypeStruct((n_batch * seq, RWKV_WIDTH), F32), jax.ShapeDtypeStruct(st_shape, F32)),
        grid=(n_batch,),
        in_specs=[
            wide(P_R), wide(P_K), wide(P_V),
            pl.BlockSpec((seq, 256), lambda b: (b, P_LORA)),
            _const_spec((16, RWKV_WIDTH)),
            _const_spec((2, 64, RWKV_WIDTH)),
            _const_spec((2, 64, RWKV_WIDTH)),
            _const_spec((128, RWKV_WIDTH)),
            pl.BlockSpec((1, 1) + st_shape[1:], lambda b: (b, init_layer, 0, 0, 0, 0)),
        ],
        out_specs=(
            pl.BlockSpec((seq, RWKV_WIDTH), lambda b: (b, 0)),
            pl.BlockSpec((1,) + st_shape[1:], lambda b: (b, 0, 0, 0, 0)),
        ),
        scratch_shapes=[pltpu.VMEM((seq, RWKV_WIDTH), F32)]
        + [pltpu.VMEM((2, seq, RWKV_WIDTH), F32)] * 4
        + [pltpu.VMEM((2, RWKV_PAIRS, 2 * RWKV_HEAD_DIM, 2 * RWKV_HEAD_DIM), F32)],
        compiler_params=_params(1),
        name="rwkv",
    )(p, p, p, p, vec, w2, a2, g2, init)


def _group_rmsnorm(x, bd, width, gain):
    ms = _dot01(x * x, bd) * (1.0 / width)
    return x * lax.rsqrt(ms + NORM_EPS) * gain


def _rope(x, tab_ref):
    half = 8
    x_next = pltpu.roll(x, ATT_WIDTH - half, 1)
    x_prev = pltpu.roll(x, half, 1)
    return x * tab_ref[0] + x_next * tab_ref[1] + x_prev * tab_ref[2]


def _attn_kernel(*refs, layer, latent):
    if latent:
        (q_ref, k_ref, v_ref, gv_ref, lam_ref, tab_ref, ck_ref, cv_ref, y_ref, ka_s, va_s, qn_s) = refs
    else:
        (q_ref, k_ref, v_ref, gv_ref, lam_ref, y_ref, kn_ref, vo_ref, ka_s, va_s, qn_s) = refs
    seq = q_ref.shape[0]
    bd32 = _block_diag_ones(ATT_WIDTH, ATT_QK_DIM)
    qn = _group_rmsnorm(q_ref[...], bd32, ATT_QK_DIM, gv_ref[0:1, :])
    kn = _group_rmsnorm(k_ref[...], bd32, ATT_QK_DIM, gv_ref[1:2, :])
    v = v_ref[...]
    if latent:
        past = ck_ref.shape[2]
        qn = _rope(qn, tab_ref)
        kn = _rope(kn, tab_ref)
        ka_s[0:past, :] = ck_ref[0, 0].astype(BF16)
        va_s[0:past, :] = cv_ref[0, 0].astype(BF16)
    else:
        past = 0
        kn_ref[...] = kn
        vo_ref[...] = v
    ka_s[past:past + seq, :] = kn.astype(BF16)
    va_s[past:past + seq, :] = v.astype(BF16)
    qn_s[...] = qn

    lam_init = 0.8 - 0.6 * math.exp(-0.3 * layer)
    lv = lam_ref[...]
    lam = (jnp.exp(jnp.sum(lv[0:1, :] * lv[1:2, :], axis=-1, keepdims=True))
           - jnp.exp(jnp.sum(lv[2:3, :] * lv[3:4, :], axis=-1, keepdims=True)) + lam_init)
    scale = ATT_QK_DIM ** -0.5
    bd64 = _block_diag_ones(LANES, ATT_V_DIM)
    lane = lax.broadcasted_iota(jnp.int32, (1, LANES), 1)
    tq = min(ATT_Q_BLOCK, seq)

    def q_block(i, carry):
        rows = pl.ds(pl.multiple_of(i * tq, tq), tq)
        outs = []
        for pair in range(ATT_HEADS // 2):
            cols = slice(pair * LANES, (pair + 1) * LANES)
            q_pair = qn_s[rows, cols]
            k_pair = ka_s[:, cols]
            v_pair = va_s[:, cols]
            o_heads = []
            for hh in range(2):
                maps = []
                for m in range(2):
                    lo = hh * 2 * ATT_QK_DIM + m * ATT_QK_DIM
                    sel = (lane >= lo) & (lane < lo + ATT_QK_DIM)
                    qm = jnp.where(sel, q_pair * scale, 0.0).astype(BF16)
                    sc = _dot_nt(qm, k_pair)
                    e = jnp.exp(sc - jnp.max(sc, axis=-1, keepdims=True))
                    maps.append(_dot(e.astype(BF16), v_pair) * (1.0 / jnp.sum(e, axis=-1, keepdims=True)))
                o_heads.append(maps[0] - lam * maps[1])
            o = jnp.where(lane < ATT_V_DIM, o_heads[0], o_heads[1])
            ms = _dot01(o * o, bd64) * (1.0 / ATT_V_DIM)
            outs.append(o * lax.rsqrt(ms + NORM_EPS) * gv_ref[2:3, cols] * (1.0 - lam_init))
        y_ref[rows, :] = jnp.concatenate(outs, axis=1)
        return carry

    lax.fori_loop(0, seq // tq, q_block, 0)


def _attn_branch(p, seq, n_batch, layer, gv, lam, rope_tab=None, cache_k=None, cache_v=None):
    latent = cache_k is not None
    blk = lambda col: pl.BlockSpec((seq, ATT_WIDTH), lambda b: (b, col))
    in_specs = [blk(P_QA), blk(P_KA), blk(P_VA), _const_spec((SUBLANES, ATT_WIDTH)), _const_spec((4, ATT_QK_DIM))]
    args = [p, p, p, gv, lam]
    seq_out = jax.ShapeDtypeStruct((n_batch * seq, ATT_WIDTH), F32)
    out_blk = pl.BlockSpec((seq, ATT_WIDTH), lambda b: (b, 0))
    if latent:
        past = cache_k.shape[2]
        in_specs += [
            _const_spec((3, seq, ATT_WIDTH)),
            pl.BlockSpec((1, 1, past, ATT_WIDTH), lambda b: (b, layer, 0, 0)),
            pl.BlockSpec((1, 1, past, ATT_WIDTH), lambda b: (b, layer, 0, 0)),
        ]
        args += [rope_tab, cache_k, cache_v]
        out_shape, out_specs = seq_out, out_blk
    else:
        past = 0
        out_shape, out_specs = (seq_out,) * 3, (out_blk,) * 3
    return pl.pallas_call(
        functools.partial(_attn_kernel, layer=layer, latent=latent),
        out_shape=out_shape,
        grid=(n_batch,),
        in_specs=in_specs,
        out_specs=out_specs,
        scratch_shapes=[
            pltpu.VMEM((past + seq, ATT_WIDTH), BF16),
            pltpu.VMEM((past + seq, ATT_WIDTH), BF16),
            pltpu.VMEM((seq, ATT_WIDTH), F32),
        ],
        compiler_params=_params(1),
        name="attn",
    )(*args)


def _route(logits):
    lane = lax.broadcasted_iota(jnp.int32, logits.shape, 1)
    big = jnp.int32(1 << 20)
    neg = -jnp.inf
    gmask = lane < N_GROUPS
    gl = jnp.where(gmask, logits, neg)
    gmax = jnp.max(gl, axis=-1, keepdims=True)
    gsum = jnp.sum(jnp.where(gmask, jnp.exp(gl - gmax), 0.0), axis=-1, keepdims=True)
    g_val = 1.0 / gsum
    g_idx = jnp.min(jnp.where(gmask & (gl == gmax), lane, big), axis=-1, keepdims=True)
    e_lane = lane - N_GROUPS
    sel = (e_lane >= 0) & (e_lane < N_EXPERTS) & ((e_lane // EXPERTS_PER_GROUP) == g_idx)
    l1 = jnp.max(jnp.where(sel, logits, neg), axis=-1, keepdims=True)
    i1 = jnp.min(jnp.where(sel & (logits == l1), lane, big), axis=-1, keepdims=True)
    sel2 = sel & (lane != i1)
    l2 = jnp.max(jnp.where(sel2, logits, neg), axis=-1, keepdims=True)
    i2 = jnp.min(jnp.where(sel2 & (logits == l2), lane, big), axis=-1, keepdims=True)
    e2 = jnp.exp(l2 - l1)
    w1 = 1.0 / (1.0 + e2)
    w2 = e2 * w1
    return jnp.where(lane == i1, g_val * w1, 0.0) + jnp.where(lane == i2, g_val * w2, 0.0)


def _post_kernel(x_ref, ys_ref, yr_ref, ya_ref, mod_ref, g_ref, wo_ref, wr_ref, br_ref, w1_ref, w3_ref, w2_ref, o_ref):
    mix = _dot(ys_ref[...].astype(BF16), wo_ref[0:SSD_WIDTH, :])
    mix = mix + _dot(yr_ref[...].astype(BF16), wo_ref[SSD_WIDTH:SSD_WIDTH + RWKV_WIDTH, :])
    mix = mix + _dot(ya_ref[...].astype(BF16), wo_ref[SSD_WIDTH + RWKV_WIDTH:, :])
    x = x_ref[...] + mod_ref[0, 2:3, :] * mix
    h = _modulated_norm(x, g_ref[...], mod_ref[0, 3:4, :], mod_ref[0, 4:5, :])
    gate = _route(_dot(h, wr_ref[...], HIGHEST) + br_ref[...])
    hb = h.astype(BF16)
    acc = jnp.zeros(x.shape, F32)
    for e in range(N_EXPERTS):
        a = _dot(hb, w1_ref[e])
        b = _dot(hb, w3_ref[e])
        hid = _silu(a) * b * gate[:, N_GROUPS + e:N_GROUPS + e + 1]
        acc = acc + _dot(hid.astype(BF16), w2_ref[e])
    o_ref[...] = x + mod_ref[0, 5:6, :] * acc


def _post(x, y_ssd, y_rwkv, y_att, mod, gain, wo, wr, br, w1, w3, w2, mod_row):
    n_tok = x.shape[0]
    tok = lambda width: pl.BlockSpec((TOKEN_TILE, width), lambda i: (i, 0))
    return pl.pallas_call(
        _post_kernel,
        out_shape=jax.ShapeDtypeStruct((n_tok, D_MODEL), F32),
        grid=(n_tok // TOKEN_TILE,),
        in_specs=[
            tok(D_MODEL), tok(SSD_WIDTH), tok(RWKV_WIDTH), tok(ATT_WIDTH),
            pl.BlockSpec((1, 6, D_MODEL), lambda i: (mod_row(i), 0, 0)),
            _const_spec((1, D_MODEL)),
            _const_spec((D_MODEL, D_MODEL)),
            _const_spec((D_MODEL, LANES)),
            _const_spec((1, LANES)),
            _const_spec((N_EXPERTS, D_MODEL, EXPERT_HIDDEN)),
            _const_spec((N_EXPERTS, D_MODEL, EXPERT_HIDDEN)),
            _const_spec((N_EXPERTS, EXPERT_HIDDEN, D_MODEL)),
        ],
        out_specs=tok(D_MODEL),
        compiler_params=_params(1),
        name="post",
    )(x, y_ssd, y_rwkv, y_att, mod, gain, wo, wr, br, w1, w3, w2)


def _pad_rows(a, rows):
    return jnp.pad(a, ((0, rows - a.shape[0]), (0, 0)))


def _reorder_w_in(w_in):
    widths = (640, 384, 12, 384, 384, 384, 64, 64, 128, 256, 256, 256)
    offs = [0]
    for w in widths:
        offs.append(offs[-1] + w)
    g = [w_in[..., offs[i]:offs[i + 1]] for i in range(len(widths))]
    xbc, z, dt, r, k, v, xw, xa, xg, qa, ka, va = g
    pad = jnp.zeros(w_in.shape[:-1] + (LANES - 12,), w_in.dtype)
    return jnp.concatenate([z, r, k, v, xw, xa, xg, qa, ka, va, xbc, dt, pad], axis=-1)


def _rope_tables(n_tokens):
    half = ATT_QK_DIM // 2
    quarter = half // 2
    tok = jnp.arange(n_tokens)
    row = (tok // GRID_W).astype(F32)
    col = (tok % GRID_W).astype(F32)
    inv = ROPE_BASE ** (-jnp.arange(quarter, dtype=F32) / quarter)
    lane = jnp.arange(ATT_WIDTH)
    within = lane % ATT_QK_DIM
    use_col = within >= half
    freq = inv[within % quarter]
    second = (within % half) >= quarter
    pos = jnp.where(use_col[None, :], col[:, None], row[:, None])
    ang = pos * freq[None, :]
    cos = jnp.cos(ang)
    sin = jnp.sin(ang)
    return jnp.stack([cos, jnp.where(second[None, :], 0.0, -sin), jnp.where(second[None, :], sin, 0.0)])


def kernel(x_prompt, x_sample, c, cache_attn_k, cache_attn_v, state_ssd, state_rwkv, c_ctx, ada_w, ada_b, norm1_g, norm2_g, w_in, ssd_conv_w, ssd_conv_b, ssd_dt_bias, ssd_a_log, ssd_d, ssd_norm_g, rwkv_w0, rwkv_w2, rwkv_a0, rwkv_a2, rwkv_g2, rwkv_kk, rwkv_ka, rwkv_rk, rwkv_ln_g, rwkv_ln_b, att_qnorm_g, att_knorm_g, att_lambda, att_onorm_g, w_out, moe_wg, moe_bg, moe_we, moe_be, moe_w1, moe_w3, moe_w2):
    n_ctx, s_ctx, _ = x_prompt.shape
    n_lat, s_lat, _ = x_sample.shape
    past = cache_attn_k.shape[2]

    cond = _pad_rows(jnp.concatenate([c_ctx[None, :], c], axis=0), MOD_ROWS)
    mod = _ada_mod(cond, ada_w, ada_b).reshape(DEPTH, MOD_ROWS, 6, D_MODEL)

    w_in_p = _reorder_w_in(w_in).astype(BF16)
    w_out_b = w_out.astype(BF16)
    w1_b, w3_b, w2_b = moe_w1.astype(BF16), moe_w3.astype(BF16), moe_w2.astype(BF16)
    w_route = jnp.concatenate(
        [moe_wg, moe_we, jnp.zeros((DEPTH, D_MODEL, LANES - N_GROUPS - N_EXPERTS), F32)], axis=-1)
    b_route = jnp.concatenate(
        [moe_bg, moe_be, jnp.zeros((DEPTH, LANES - N_GROUPS - N_EXPERTS), F32)], axis=-1)[:, None, :]
    rope_tab = _rope_tables(s_lat)
    cache_k = cache_attn_k.reshape(n_lat, DEPTH, past, ATT_WIDTH)
    cache_v = cache_attn_v.reshape(n_lat, DEPTH, past, ATT_WIDTH)
    zero_ssd = jnp.zeros((n_ctx, 1, 2, SSD_HEADS, SSD_HEAD_DIM, SSD_STATE), F32)
    zero_rwkv = jnp.zeros((n_ctx, 1, 2, RWKV_HEADS, RWKV_HEAD_DIM, RWKV_HEAD_DIM), F32)

    ctx_row = lambda i: 0
    lat_row = lambda i: 1 + i // (s_lat // TOKEN_TILE)

    xp = x_prompt.reshape(n_ctx * s_ctx, D_MODEL)
    xs = x_sample.reshape(n_lat * s_lat, D_MODEL)
    new_k, new_v, new_ssd, new_rwkv = [], [], [], []
    for l in range(DEPTH):
        conv = _pad_rows(jnp.concatenate([ssd_conv_w[l], ssd_conv_b[l][None, :]], axis=0), SUBLANES)
        ssd_vec = _pad_rows(jnp.stack([jnp.repeat(ssd_d[l], SSD_HEAD_DIM), ssd_norm_g[l]]), SUBLANES)
        dtp = _pad_rows(jnp.pad(jnp.stack([ssd_dt_bias[l].reshape(-1), ssd_a_log[l].reshape(-1)]),
                                ((0, 0), (0, LANES - 2 * SSD_HEADS))), SUBLANES)
        rwkv_vec = _pad_rows(jnp.stack([
            rwkv_kk[l], rwkv_ka[l], rwkv_rk[l].reshape(-1), rwkv_ln_g[l], rwkv_ln_b[l],
            rwkv_w0[l, 0], rwkv_w0[l, 1], rwkv_a0[l, 0], rwkv_a0[l, 1]]), 16)
        tile32 = lambda g: jnp.tile(g, ATT_WIDTH // ATT_QK_DIM)
        att_vec = _pad_rows(jnp.stack([tile32(att_qnorm_g[l]), tile32(att_knorm_g[l]),
                                       jnp.tile(att_onorm_g[l], ATT_WIDTH // ATT_V_DIM)]), SUBLANES)
        g1 = norm1_g[l][None, :]
        g2 = norm2_g[l][None, :]
        post_w = (mod[l], g2, w_out_b[l], w_route[l], b_route[l], w1_b[l], w3_b[l], w2_b[l])

        p = _in_proj(xp, mod[l], g1, w_in_p[l], ctx_row)
        y_ssd, st_ssd = _ssd_branch(p, s_ctx, n_ctx, conv, ssd_vec, dtp, zero_ssd, 0)
        y_rwkv, st_rwkv = _rwkv_branch(p, s_ctx, n_ctx, rwkv_vec, rwkv_w2[l], rwkv_a2[l], rwkv_g2[l], zero_rwkv, 0)
        y_att, k_att, v_att = _attn_branch(p, s_ctx, n_ctx, l, att_vec, att_lambda[l])
        xp = _post(xp, y_ssd, y_rwkv, y_att, *post_w, ctx_row)
        new_k.append(k_att.reshape(n_ctx, s_ctx, ATT_HEADS, 2, ATT_QK_DIM))
        new_v.append(v_att.reshape(n_ctx, s_ctx, ATT_HEADS, ATT_V_DIM))
        new_ssd.append(st_ssd)
        new_rwkv.append(st_rwkv)

        p = _in_proj(xs, mod[l], g1, w_in_p[l], lat_row)
        y_ssd, _ = _ssd_branch(p, s_lat, n_lat, conv, ssd_vec, dtp, state_ssd, l)
        y_rwkv, _ = _rwkv_branch(p, s_lat, n_lat, rwkv_vec, rwkv_w2[l], rwkv_a2[l], rwkv_g2[l], state_rwkv, l)
        y_att = _attn_branch(p, s_lat, n_lat, l, att_vec, att_lambda[l], rope_tab, cache_k, cache_v)
        xs = _post(xs, y_ssd, y_rwkv, y_att, *post_w, lat_row)

    return (xp.reshape(n_ctx, s_ctx, D_MODEL), xs.reshape(n_lat, s_lat, D_MODEL),
            jnp.stack(new_k, axis=1), jnp.stack(new_v, axis=1),
            jnp.stack(new_ssd, axis=1), jnp.stack(new_rwkv, axis=1))
```

```python
import functools
import math

import jax
import jax.numpy as jnp
from jax import lax
from jax.experimental import pallas as pl
from jax.experimental.pallas import tpu as pltpu

F32 = jnp.float32
BF16 = jnp.bfloat16
HIGHEST = lax.Precision.HIGHEST

D_MODEL = 1024
DEPTH = 2
GRID_W = 64
SSD_WIDTH = 384
SSD_HEADS = 6
SSD_GROUPS = 2
SSD_STATE = 64
SSD_HEAD_DIM = 64
SSD_CONV = 5
SSD_CONV_DIM = 640
SSD_CHUNK = 128
RWKV_WIDTH = 384
RWKV_HEADS = 6
RWKV_HEAD_DIM = 64
RWKV_CHUNK = 64
RWKV_SUB = 8
RWKV_PAIRS = RWKV_HEADS // 2
RWKV_GN_EPS = 64e-5
ATT_WIDTH = 256
ATT_HEADS = 4
ATT_QK_DIM = 32
ATT_V_DIM = 64
ATT_Q_BLOCK = 256
ROPE_BASE = 10000.0
N_GROUPS = 4
EXPERTS_PER_GROUP = 4
N_EXPERTS = 16
EXPERT_HIDDEN = 256
NORM_EPS = 1e-6
NORM_TERMS = 2

LANES = 128
SUBLANES = 8
TOKEN_TILE = 512
VMEM_LIMIT_BYTES = 56 * 1024 * 1024

P_WIDTH = 3328
P_Z, P_R, P_K, P_V = 0, 1, 2, 3
P_LORA, P_QA, P_KA, P_VA = 6, 7, 8, 9
P_XBC = 4
P_DT = 25
MOD_ROWS = 16


def _dot(a, b, precision=None):
    return lax.dot_general(a, b, (((1,), (0,)), ((), ())), precision=precision, preferred_element_type=F32)


def _dot_nt(a, b, precision=None):
    return lax.dot_general(a, b, (((1,), (1,)), ((), ())), precision=precision, preferred_element_type=F32)


def _dot_tn(a, b, precision=None):
    return lax.dot_general(a, b, (((0,), (0,)), ((), ())), precision=precision, preferred_element_type=F32)


def _bdot(a, b):
    return _dot(a.astype(BF16), b.astype(BF16))


def _bdot_nt(a, b):
    return _dot_nt(a.astype(BF16), b.astype(BF16))


def _bdot_tn(a, b):
    return _dot_tn(a.astype(BF16), b.astype(BF16))


def _bf16_terms(a, terms):
    parts = []
    for _ in range(terms):
        part = a.astype(BF16)
        a = a - part.astype(F32)
        parts.append(part)
    return parts


def _dot01(a, b01, terms=3):
    b = b01.astype(BF16)
    return sum(_dot(part, b) for part in _bf16_terms(a, terms))


def _dot_x3(a, b):
    a_hi, a_lo = _bf16_terms(a, 2)
    b_hi, b_lo = _bf16_terms(b, 2)
    return _dot(a_hi, b_hi) + (_dot(a_lo, b_hi) + _dot(a_hi, b_lo))


def _dot01_left(a01, b, terms=3):
    a = a01.astype(BF16)
    return sum(_dot(a, part) for part in _bf16_terms(b, terms))


def _sigmoid(x):
    return 0.5 * jnp.tanh(0.5 * x) + 0.5


def _silu(x):
    return x * _sigmoid(x)


def _softplus(x):
    return jnp.maximum(x, 0.0) + jnp.log1p(jnp.exp(-jnp.abs(x)))


def _block_diag_ones(n, blk):
    r = lax.broadcasted_iota(jnp.int32, (n, n), 0) // blk
    c = lax.broadcasted_iota(jnp.int32, (n, n), 1) // blk
    return (r == c).astype(F32)


def _const_spec(shape):
    zeros = (0,) * len(shape)
    return pl.BlockSpec(shape, lambda *_: zeros, pipeline_mode=pl.Buffered(1))


def _layer_spec(shape, layer):
    idx = (layer,) + (0,) * len(shape)
    return pl.BlockSpec((None,) + tuple(shape), lambda *_: idx, pipeline_mode=pl.Buffered(1))


def _mod_spec(mod_row, layer):
    return pl.BlockSpec((None, 1, 6, D_MODEL), lambda i: (layer, mod_row(i), 0, 0))


def _params(n_axes=1):
    return pltpu.CompilerParams(dimension_semantics=("arbitrary",) * n_axes, vmem_limit_bytes=VMEM_LIMIT_BYTES)


def _ada_kernel(c_ref, w_ref, b_ref, o_ref):
    o_ref[0] = _dot(_silu(c_ref[...]), w_ref[0], HIGHEST) + b_ref[0]


def _ada_mod(cond, ada_w, ada_b):
    tn = 1024
    n = 6 * D_MODEL
    return pl.pallas_call(
        _ada_kernel,
        out_shape=jax.ShapeDtypeStruct((DEPTH, MOD_ROWS, n), F32),
        grid=(DEPTH, n // tn),
        in_specs=[
            pl.BlockSpec((MOD_ROWS, D_MODEL), lambda l, j: (0, 0)),
            pl.BlockSpec((1, D_MODEL, tn), lambda l, j: (l, 0, j)),
            pl.BlockSpec((1, 1, tn), lambda l, j: (l, 0, j)),
        ],
        out_specs=pl.BlockSpec((1, MOD_ROWS, tn), lambda l, j: (l, 0, j)),
        compiler_params=_params(2),
        name="ada_mod",
    )(cond, ada_w, ada_b.reshape(DEPTH, 1, n))


def _modulated_norm(x, gain, shift, scale):
    y = x * lax.rsqrt(jnp.mean(x * x, axis=-1, keepdims=True) + NORM_EPS)
    return y * gain * (1.0 + scale) + shift


def _in_proj_kernel(x_ref, mod_ref, g_ref, w_ref, o_ref):
    h = _modulated_norm(x_ref[...], g_ref[...], mod_ref[0, 0:1, :], mod_ref[0, 1:2, :]).astype(BF16)
    cw = 512
    for c0 in range(0, P_WIDTH, cw):
        c1 = min(c0 + cw, P_WIDTH)
        o_ref[:, c0:c1] = _dot(h, w_ref[:, c0:c1])


def _in_proj(x, mod, gain, w, mod_row, layer):
    n_tok = x.shape[0]
    return pl.pallas_call(
        _in_proj_kernel,
        out_shape=jax.ShapeDtypeStruct((n_tok, P_WIDTH), F32),
        grid=(n_tok // TOKEN_TILE,),
        in_specs=[
            pl.BlockSpec((TOKEN_TILE, D_MODEL), lambda i: (i, 0)),
            _mod_spec(mod_row, layer),
            _const_spec((1, D_MODEL)),
            _layer_spec((D_MODEL, P_WIDTH), layer),
        ],
        out_specs=pl.BlockSpec((TOKEN_TILE, P_WIDTH), lambda i: (i, 0)),
        compiler_params=_params(1),
        name="in_proj",
    )(x, mod, gain, w)


def _ssd_kernel(xbc_ref, z_ref, dt_ref, cw_ref, vec_ref, dtp_ref, init_ref, y_ref, fin_ref,
                pad_ref, xc_ref, dt_s, da_s, y_s, st_ref):
    seq = xbc_ref.shape[0]
    q = SSD_CHUNK
    n_chunks = seq // q
    rep = SSD_HEADS // SSD_GROUPS
    pad_ref[0:SUBLANES, :] = jnp.zeros((SUBLANES, SSD_CONV_DIM), F32)
    pad_ref[seq + SUBLANES:seq + 2 * SUBLANES, :] = jnp.zeros((SUBLANES, SSD_CONV_DIM), F32)
    pad_ref[SUBLANES:seq + SUBLANES, :] = xbc_ref[...]
    for r0 in range(0, seq, q):
        acc = jnp.zeros((q, SSD_CONV_DIM), F32) + cw_ref[SSD_CONV:SSD_CONV + 1, :]
        for j in range(SSD_CONV):
            off = r0 + SUBLANES + j - SSD_CONV // 2
            acc = acc + cw_ref[j:j + 1, :] * pad_ref[off:off + q, :]
        xc_ref[r0:r0 + q, :] = _silu(acc)
    dt = _softplus(dt_ref[...] + dtp_ref[0:1, :])
    dt_s[...] = dt
    da_s[...] = dt * (-jnp.exp(dtp_ref[1:2, :]))

    row = lax.broadcasted_iota(jnp.int32, (q, q), 0)
    col = lax.broadcasted_iota(jnp.int32, (q, q), 1)
    keep = [col <= row, col >= row]
    tri = [k.astype(F32) for k in keep]
    wide = SSD_HEADS * SSD_HEAD_DIM
    src_c = lax.broadcasted_iota(jnp.int32, (LANES, wide), 0)
    head_c = lax.broadcasted_iota(jnp.int32, (LANES, wide), 1) // SSD_HEAD_DIM
    to_chan = [(src_c == d * SSD_HEADS + head_c).astype(F32) for d in range(2)]
    grp_ok = (src_c // SSD_STATE) == (head_c // rep)
    lane = lax.broadcasted_iota(jnp.int32, (q, LANES), 1)
    zero_pn = jnp.zeros((SSD_HEAD_DIM, SSD_STATE), F32)
    for d in range(2):
        rows_pn = []
        for h in range(SSD_HEADS):
            s = init_ref[0, 0, d, h]
            halves = [s, zero_pn] if h // rep == 0 else [zero_pn, s]
            rows_pn.append(jnp.concatenate(halves, axis=1))
        st_ref[d] = jnp.concatenate(rows_pn, axis=0).T

    def chunk_body(i, carry):
        for d in range(2):
            c = i if d == 0 else n_chunks - 1 - i
            rows = pl.ds(pl.multiple_of(c * q, q), q)
            last = q - 1 if d == 0 else 0
            da = da_s[rows, :]
            g = _dot01_left(tri[d], da)
            g_t = g.T
            chan = _dot01(jnp.concatenate([g, dt_s[rows, :]], axis=0), to_chan[d])
            g_chan = chan[0:q]
            dt_chan = chan[q:]
            tot_chan = g_chan[last:last + 1, :]
            xcc = xc_ref[rows, :]
            x_all = xcc[:, 0:wide] * dt_chan
            x_dec = x_all * jnp.exp(tot_chan - g_chan)
            bw = xcc[:, wide:wide + LANES].astype(BF16)
            cw = xcc[:, wide + LANES:wide + 2 * LANES]
            cwb = cw.astype(BF16)
            cb = [_dot_nt(jnp.where(lane < SSD_STATE, cw, 0.0).astype(BF16), bw),
                  _dot_nt(jnp.where(lane < SSD_STATE, 0.0, cw).astype(BF16), bw)]
            tiles = []
            for h in range(SSD_HEADS):
                cidx = d * SSD_HEADS + h
                diff = g[:, cidx:cidx + 1] - g_t[cidx:cidx + 1, :]
                lm = jnp.exp(jnp.where(keep[d], diff, -jnp.inf))
                tiles.append((cb[h // rep] * lm).astype(BF16))
            y_diag = []
            for pr in range(SSD_HEADS // 2):
                x_pair = x_all[:, pr * LANES:(pr + 1) * LANES]
                x_bd = jnp.concatenate([jnp.where(lane < SSD_HEAD_DIM, x_pair, 0.0),
                                        jnp.where(lane < SSD_HEAD_DIM, 0.0, x_pair)], axis=0).astype(BF16)
                y_diag.append(_dot(jnp.concatenate(tiles[2 * pr:2 * pr + 2], axis=1), x_bd))
            st = st_ref[d]
            y_s[d, rows, :] = jnp.concatenate(y_diag, axis=1) + _dot(cwb, st.astype(BF16)) * jnp.exp(g_chan)
            st_new = jnp.exp(tot_chan) * st + _dot_tn(bw, x_dec.astype(BF16))
            st_ref[d] = jnp.where(grp_ok, st_new, 0.0)
        return carry

    lax.fori_loop(0, n_chunks, chunk_body, 0)
    for d in range(2):
        st = st_ref[d].T
        for h in range(SSD_HEADS):
            grp = h // rep
            fin_ref[0, d, h] = st[h * SSD_HEAD_DIM:(h + 1) * SSD_HEAD_DIM, grp * SSD_STATE:(grp + 1) * SSD_STATE]

    y = y_s[0] + y_s[1] + vec_ref[0:1, :] * xc_ref[:, 0:SSD_WIDTH]
    y = y * _silu(z_ref[...])
    y = y * lax.rsqrt(jnp.mean(y * y, axis=-1, keepdims=True) + NORM_EPS)
    y_ref[...] = y * vec_ref[1:2, :]


def _ssd_branch(p, seq, n_batch, conv, vec, dtp, init, init_layer):
    st_shape = (n_batch, 2, SSD_HEADS, SSD_HEAD_DIM, SSD_STATE)
    return pl.pallas_call(
        _ssd_kernel,
        out_shape=(jax.ShapeDtypeStruct((n_batch * seq, SSD_WIDTH), F32), jax.ShapeDtypeStruct(st_shape, F32)),
        grid=(n_batch,),
        in_specs=[
            pl.BlockSpec((seq, SSD_CONV_DIM), lambda b: (b, P_XBC)),
            pl.BlockSpec((seq, SSD_WIDTH), lambda b: (b, P_Z)),
            pl.BlockSpec((seq, LANES), lambda b: (b, P_DT)),
            _const_spec((SUBLANES, SSD_CONV_DIM)),
            _const_spec((SUBLANES, SSD_WIDTH)),
            _const_spec((SUBLANES, LANES)),
            pl.BlockSpec((1, 1) + st_shape[1:], lambda b: (b, init_layer, 0, 0, 0, 0)),
        ],
        out_specs=(
            pl.BlockSpec((seq, SSD_WIDTH), lambda b: (b, 0)),
            pl.BlockSpec((1,) + st_shape[1:], lambda b: (b, 0, 0, 0, 0)),
        ),
        scratch_shapes=[
            pltpu.VMEM((seq + 2 * SUBLANES, SSD_CONV_DIM), F32),
            pltpu.VMEM((seq, SSD_CONV_DIM), F32),
            pltpu.VMEM((seq, LANES), F32),
            pltpu.VMEM((seq, LANES), F32),
            pltpu.VMEM((2, seq, SSD_WIDTH), F32),
            pltpu.VMEM((2, 2 * SSD_STATE, SSD_HEADS * SSD_HEAD_DIM), F32),
        ],
        compiler_params=_params(1),
        name="ssd",
    )(p, p, p, conv, vec, dtp, init)


def _rwkv_kernel(r_ref, k_ref, v_ref, lora_ref, vec_ref, w2_ref, a2_ref, g2_ref, init_ref, y_ref, fin_ref,
                 kkn_s, lw_s, kd_s, kka_s, y_s, st_ref):
    seq = r_ref.shape[0]
    t = RWKV_CHUNK
    n_chunks = seq // t
    hd = RWKV_HEAD_DIM
    n2 = 2 * t
    n4 = 4 * t
    r = r_ref[...]
    k = k_ref[...]
    lora = lora_ref[...]
    xw = lora[:, 0:64]
    xa = lora[:, 64:128]
    xg = lora[:, 128:256]
    bd = _block_diag_ones(RWKV_WIDTH, hd)
    kkp = k * vec_ref[0:1, :]
    kss = _dot01(kkp * kkp, bd, NORM_TERMS)
    kkn = kkp * lax.rsqrt(jnp.maximum(kss, 1e-24))
    kkn_s[...] = kkn
    tw = jnp.tanh(xw)
    zero_hh = jnp.zeros((hd, hd), F32)
    for d in range(2):
        wl = vec_ref[5 + d:6 + d, :] + _bdot(tw, w2_ref[d])
        lw_s[d] = -math.exp(-0.5) * _sigmoid(wl)
        a = _sigmoid(vec_ref[7 + d:8 + d, :] + _bdot(xa, a2_ref[d]))
        kd_s[d] = k * (1.0 + (a - 1.0) * vec_ref[1:2, :])
        kka_s[d] = kkn * a
        for p in range(RWKV_PAIRS):
            st_ref[d, p] = jnp.concatenate([
                jnp.concatenate([init_ref[0, 0, d, 2 * p], zero_hh], axis=1),
                jnp.concatenate([zero_hh, init_ref[0, 0, d, 2 * p + 1]], axis=1)], axis=0)

    ri = lax.broadcasted_iota(jnp.int32, (n4, n4), 0)
    ci = lax.broadcasted_iota(jnp.int32, (n4, n4), 1)
    rt = ri % t
    ct = ci % t
    top = ri < n2
    keep = [(top & (ct < rt)) | (~top & (ct <= rt)), (top & (ct > rt)) | (~top & (ct >= rt))]
    si = lax.broadcasted_iota(jnp.int32, (n2, n2), 0) // RWKV_SUB
    sj = lax.broadcasted_iota(jnp.int32, (n2, n2), 1) // RWKV_SUB
    sub_diag = si == sj
    tr = lax.broadcasted_iota(jnp.int32, (t, t), 0)
    tc = lax.broadcasted_iota(jnp.int32, (t, t), 1)
    tri = [(tc <= tr).astype(F32), (tc >= tr).astype(F32)]
    lo = lax.broadcasted_iota(jnp.int32, (t, LANES), 1) < hd

    def block_diag(x):
        return jnp.concatenate([jnp.where(lo, x, 0.0), jnp.where(lo, 0.0, x)], axis=0)

    def chunk_body(i, carry):
        ch = []
        for d in range(2):
            c = i if d == 0 else n_chunks - 1 - i
            rows = pl.ds(pl.multiple_of(c * t, t), t)
            lw = lw_s[d, rows, :]
            g = _dot01_left(tri[d], lw)
            e_g = jnp.exp(g)
            e_ng = jnp.exp(-g)
            kap = kkn_s[rows, :] * jnp.exp(g - lw)
            rr = r_ref[rows, :] * e_g
            bb = kka_s[d, rows, :] * e_ng
            kk = kd_s[d, rows, :] * e_ng
            vv = v_ref[rows, :]
            e_tot = jnp.exp(jnp.sum(lw, axis=0, keepdims=True))
            for p in range(RWKV_PAIRS):
                sl = slice(p * LANES, (p + 1) * LANES)
                ch.append(dict(
                    d=d, p=p, rows=rows,
                    lq=jnp.concatenate([block_diag(kap[:, sl]), block_diag(rr[:, sl])], axis=0).astype(BF16),
                    rq=jnp.concatenate([block_diag(bb[:, sl]), block_diag(kk[:, sl])], axis=0).astype(BF16),
                    vb=block_diag(vv[:, sl]),
                    s0=st_ref[d, p], etot=e_tot[:, sl]))
        for x in ch:
            x["mm"] = jnp.where(keep[x["d"]], _dot_nt(x["lq"], x["rq"]), 0.0)
            x["c0"] = _bdot_nt(x["lq"], x["s0"])
        for x in ch:
            x["base"] = x["c0"] + _bdot(x["mm"][:, n2:], x["vb"])
            ab = x["mm"][0:n2, 0:n2]
            x["nd"] = jnp.where(sub_diag, ab, 0.0)
            x["rhs"] = jnp.concatenate([jnp.where(sub_diag, 0.0, ab), x["base"][0:n2]], axis=1)
        for x in ch:
            x["nd2"] = _bdot(x["nd"], x["nd"])
            x["y"] = x["rhs"] - _bdot(x["nd"], x["rhs"])
        for x in ch:
            x["nd4"] = _bdot(x["nd2"], x["nd2"])
            x["y"] = x["y"] + _bdot(x["nd2"], x["y"])
        for x in ch:
            x["y"] = x["y"] + _bdot(x["nd4"], x["y"])
            x["m"] = x["y"][:, 0:n2]
            x["x"] = x["y"][:, n2:]
        for x in ch:
            x["m2"] = _bdot(x["m"], x["m"])
            x["x"] = x["x"] - _bdot(x["m"], x["x"])
        for x in ch:
            x["m4"] = _bdot(x["m2"], x["m2"])
            x["x"] = x["x"] + _bdot(x["m2"], x["x"])
        for x in ch:
            x["x"] = x["x"] + _bdot(x["m4"], x["x"])
            x["zu"] = -x["x"]
        for x in ch:
            yb = x["base"][n2:] + _bdot(x["mm"][n2:, 0:n2], x["zu"])
            x["yo"] = yb[0:t] + yb[t:]
            uv = jnp.concatenate([x["zu"], x["vb"]], axis=0)
            st_ref[x["d"], x["p"]] = (x["s0"] + _bdot_tn(uv, x["rq"])) * x["etot"]
        for d in range(2):
            rows = ch[d * RWKV_PAIRS]["rows"]
            y_s[d, rows, :] = jnp.concatenate([x["yo"] for x in ch if x["d"] == d], axis=1)
        return carry

    lax.fori_loop(0, n_chunks, chunk_body, 0)
    for d in range(2):
        for p in range(RWKV_PAIRS):
            s = st_ref[d, p]
            fin_ref[0, d, 2 * p] = s[0:hd, 0:hd]
            fin_ref[0, d, 2 * p + 1] = s[hd:, hd:]

    yf = y_s[0] + y_s[1]
    inv = 1.0 / hd
    mu = _dot01(yf, bd, NORM_TERMS) * inv
    yc = yf - mu
    var = _dot01(yc * yc, bd, NORM_TERMS) * inv
    yn = yc * lax.rsqrt(var + RWKV_GN_EPS) * vec_ref[3:4, :] + vec_ref[4:5, :]
    v = v_ref[...]
    bonus = _dot01(r * k * vec_ref[2:3, :], bd, NORM_TERMS) * v
    gate = _bdot(_sigmoid(xg), g2_ref[...])
    y_ref[...] = (yn + bonus) * gate


def _rwkv_branch(p, seq, n_batch, vec, w2, a2, g2, layer, init, init_layer):
    st_shape = (n_batch, 2, RWKV_HEADS, RWKV_HEAD_DIM, RWKV_HEAD_DIM)
    wide = lambda col: pl.BlockSpec((seq, RWKV_WIDTH), lambda b: (b, col))
    return pl.pallas_call(
        _rwkv_kernel,
        out_shape=(jax.ShapeDtypeStruct((n_batch * seq, RWKV_WIDTH), F32), jax.ShapeDtypeStruct(st_shape, F32)),
        grid=(n_batch,),
        in_specs=[
            wide(P_R), wide(P_K), wide(P_V),
            pl.BlockSpec((seq, 256), lambda b: (b, P_LORA)),
            _const_spec((16, RWKV_WIDTH)),
            _layer_spec((2, 64, RWKV_WIDTH), layer),
            _layer_spec((2, 64, RWKV_WIDTH), layer),
            _layer_spec((128, RWKV_WIDTH), layer),
            pl.BlockSpec((1, 1) + st_shape[1:], lambda b: (b, init_layer, 0, 0, 0, 0)),
        ],
        out_specs=(
            pl.BlockSpec((seq, RWKV_WIDTH), lambda b: (b, 0)),
            pl.BlockSpec((1,) + st_shape[1:], lambda b: (b, 0, 0, 0, 0)),
        ),
        scratch_shapes=[pltpu.VMEM((seq, RWKV_WIDTH), F32)]
        + [pltpu.VMEM((2, seq, RWKV_WIDTH), F32)] * 4
        + [pltpu.VMEM((2, RWKV_PAIRS, 2 * RWKV_HEAD_DIM, 2 * RWKV_HEAD_DIM), F32)],
        compiler_params=_params(1),
        name="rwkv",
    )(p, p, p, p, vec, w2, a2, g2, init)


def _group_rmsnorm(x, bd, width, gain):
    ms = _dot01(x * x, bd, NORM_TERMS) * (1.0 / width)
    return x * lax.rsqrt(ms + NORM_EPS) * gain


def _rope(x, tab_ref):
    half = 8
    x_next = pltpu.roll(x, ATT_WIDTH - half, 1)
    x_prev = pltpu.roll(x, half, 1)
    return x * tab_ref[0] + x_next * tab_ref[1] + x_prev * tab_ref[2]


def _attn_kernel(*refs, layer, latent):
    if latent:
        (q_ref, k_ref, v_ref, gv_ref, lam_ref, tab_ref, ck_ref, cv_ref, y_ref, ka_s, va_s, qn_s) = refs
    else:
        (q_ref, k_ref, v_ref, gv_ref, lam_ref, y_ref, kn_ref, vo_ref, ka_s, va_s, qn_s) = refs
    seq = q_ref.shape[0]
    bd32 = _block_diag_ones(ATT_WIDTH, ATT_QK_DIM)
    qn = _group_rmsnorm(q_ref[...], bd32, ATT_QK_DIM, gv_ref[0:1, :])
    kn = _group_rmsnorm(k_ref[...], bd32, ATT_QK_DIM, gv_ref[1:2, :])
    v = v_ref[...]
    if latent:
        past = ck_ref.shape[2]
        qn = _rope(qn, tab_ref)
        kn = _rope(kn, tab_ref)
        ka_s[0:past, :] = ck_ref[0, 0].astype(BF16)
        va_s[0:past, :] = cv_ref[0, 0].astype(BF16)
    else:
        past = 0
        kn_ref[...] = kn
        vo_ref[...] = v
    ka_s[past:past + seq, :] = kn.astype(BF16)
    va_s[past:past + seq, :] = v.astype(BF16)
    qn_s[...] = qn

    lam_init = 0.8 - 0.6 * math.exp(-0.3 * layer)
    lv = lam_ref[...]
    lam = (jnp.exp(jnp.sum(lv[0:1, :] * lv[1:2, :], axis=-1, keepdims=True))
           - jnp.exp(jnp.sum(lv[2:3, :] * lv[3:4, :], axis=-1, keepdims=True)) + lam_init)
    scale = ATT_QK_DIM ** -0.5
    bd64 = _block_diag_ones(LANES, ATT_V_DIM)
    lane = lax.broadcasted_iota(jnp.int32, (1, LANES), 1)
    tq = min(ATT_Q_BLOCK, seq)

    def q_block(i, carry):
        rows = pl.ds(pl.multiple_of(i * tq, tq), tq)
        outs = []
        for pair in range(ATT_HEADS // 2):
            cols = slice(pair * LANES, (pair + 1) * LANES)
            q_pair = qn_s[rows, cols]
            k_pair = ka_s[:, cols]
            v_pair = va_s[:, cols]
            o_heads = []
            for hh in range(2):
                wgt = None
                for m in range(2):
                    lo = hh * 2 * ATT_QK_DIM + m * ATT_QK_DIM
                    sel = (lane >= lo) & (lane < lo + ATT_QK_DIM)
                    qm = jnp.where(sel, q_pair * scale, 0.0).astype(BF16)
                    sc = _dot_nt(qm, k_pair)
                    e = jnp.exp(sc - jnp.max(sc, axis=-1, keepdims=True))
                    inv = 1.0 / jnp.sum(e, axis=-1, keepdims=True)
                    wgt = e * inv if m == 0 else wgt - e * (lam * inv)
                o_heads.append(_dot(wgt.astype(BF16), v_pair))
            o = jnp.where(lane < ATT_V_DIM, o_heads[0], o_heads[1])
            ms = _dot01(o * o, bd64, NORM_TERMS) * (1.0 / ATT_V_DIM)
            outs.append(o * lax.rsqrt(ms + NORM_EPS) * gv_ref[2:3, cols] * (1.0 - lam_init))
        y_ref[rows, :] = jnp.concatenate(outs, axis=1)
        return carry

    lax.fori_loop(0, seq // tq, q_block, 0)


def _attn_branch(p, seq, n_batch, layer, gv, lam, rope_tab=None, cache_k=None, cache_v=None):
    latent = cache_k is not None
    blk = lambda col: pl.BlockSpec((seq, ATT_WIDTH), lambda b: (b, col))
    in_specs = [blk(P_QA), blk(P_KA), blk(P_VA), _const_spec((SUBLANES, ATT_WIDTH)), _const_spec((4, ATT_QK_DIM))]
    args = [p, p, p, gv, lam]
    seq_out = jax.ShapeDtypeStruct((n_batch * seq, ATT_WIDTH), F32)
    out_blk = pl.BlockSpec((seq, ATT_WIDTH), lambda b: (b, 0))
    if latent:
        past = cache_k.shape[2]
        in_specs += [
            _const_spec((3, seq, ATT_WIDTH)),
            pl.BlockSpec((1, 1, past, ATT_WIDTH), lambda b: (b, layer, 0, 0)),
            pl.BlockSpec((1, 1, past, ATT_WIDTH), lambda b: (b, layer, 0, 0)),
        ]
        args += [rope_tab, cache_k, cache_v]
        out_shape, out_specs = seq_out, out_blk
    else:
        past = 0
        out_shape, out_specs = (seq_out,) * 3, (out_blk,) * 3
    return pl.pallas_call(
        functools.partial(_attn_kernel, layer=layer, latent=latent),
        out_shape=out_shape,
        grid=(n_batch,),
        in_specs=in_specs,
        out_specs=out_specs,
        scratch_shapes=[
            pltpu.VMEM((past + seq, ATT_WIDTH), BF16),
            pltpu.VMEM((past + seq, ATT_WIDTH), BF16),
            pltpu.VMEM((seq, ATT_WIDTH), F32),
        ],
        compiler_params=_params(1),
        name="attn",
    )(*args)


def _route(logits):
    lane = lax.broadcasted_iota(jnp.int32, logits.shape, 1)
    big = jnp.int32(1 << 20)
    neg = -jnp.inf
    gmask = lane < N_GROUPS
    gl = jnp.where(gmask, logits, neg)
    gmax = jnp.max(gl, axis=-1, keepdims=True)
    gsum = jnp.sum(jnp.where(gmask, jnp.exp(gl - gmax), 0.0), axis=-1, keepdims=True)
    g_val = 1.0 / gsum
    g_idx = jnp.min(jnp.where(gmask & (gl == gmax), lane, big), axis=-1, keepdims=True)
    e_lane = lane - N_GROUPS
    sel = (e_lane >= 0) & (e_lane < N_EXPERTS) & ((e_lane // EXPERTS_PER_GROUP) == g_idx)
    l1 = jnp.max(jnp.where(sel, logits, neg), axis=-1, keepdims=True)
    i1 = jnp.min(jnp.where(sel & (logits == l1), lane, big), axis=-1, keepdims=True)
    sel2 = sel & (lane != i1)
    l2 = jnp.max(jnp.where(sel2, logits, neg), axis=-1, keepdims=True)
    i2 = jnp.min(jnp.where(sel2 & (logits == l2), lane, big), axis=-1, keepdims=True)
    e2 = jnp.exp(l2 - l1)
    w1 = 1.0 / (1.0 + e2)
    w2 = e2 * w1
    return jnp.where(lane == i1, g_val * w1, 0.0) + jnp.where(lane == i2, g_val * w2, 0.0)


def _post_kernel(x_ref, ys_ref, yr_ref, ya_ref, mod_ref, g_ref, wo_ref, wr_ref, br_ref, w1_ref, w3_ref, w2_ref, o_ref):
    mix = _dot(ys_ref[...].astype(BF16), wo_ref[0:SSD_WIDTH, :])
    mix = mix + _dot(yr_ref[...].astype(BF16), wo_ref[SSD_WIDTH:SSD_WIDTH + RWKV_WIDTH, :])
    mix = mix + _dot(ya_ref[...].astype(BF16), wo_ref[SSD_WIDTH + RWKV_WIDTH:, :])
    x = x_ref[...] + mod_ref[0, 2:3, :] * mix
    h = _modulated_norm(x, g_ref[...], mod_ref[0, 3:4, :], mod_ref[0, 4:5, :])
    gate = _route(_dot_x3(h, wr_ref[...]) + br_ref[...])
    hb = h.astype(BF16)
    acc = jnp.zeros(x.shape, F32)
    for e in range(N_EXPERTS):
        a = _dot(hb, w1_ref[e])
        b = _dot(hb, w3_ref[e])
        hid = _silu(a) * b * gate[:, N_GROUPS + e:N_GROUPS + e + 1]
        acc = acc + _dot(hid.astype(BF16), w2_ref[e])
    o_ref[...] = x + mod_ref[0, 5:6, :] * acc


def _post(x, y_ssd, y_rwkv, y_att, mod, gain, wo, wr, br, w1, w3, w2, mod_row, layer):
    n_tok = x.shape[0]
    tok = lambda width: pl.BlockSpec((TOKEN_TILE, width), lambda i: (i, 0))
    return pl.pallas_call(
        _post_kernel,
        out_shape=jax.ShapeDtypeStruct((n_tok, D_MODEL), F32),
        grid=(n_tok // TOKEN_TILE,),
        in_specs=[
            tok(D_MODEL), tok(SSD_WIDTH), tok(RWKV_WIDTH), tok(ATT_WIDTH),
            _mod_spec(mod_row, layer),
            _const_spec((1, D_MODEL)),
            _layer_spec((D_MODEL, D_MODEL), layer),
            _layer_spec((D_MODEL, LANES), layer),
            _layer_spec((1, LANES), layer),
            _layer_spec((N_EXPERTS, D_MODEL, EXPERT_HIDDEN), layer),
            _layer_spec((N_EXPERTS, D_MODEL, EXPERT_HIDDEN), layer),
            _layer_spec((N_EXPERTS, EXPERT_HIDDEN, D_MODEL), layer),
        ],
        out_specs=tok(D_MODEL),
        compiler_params=_params(1),
        name="post",
    )(x, y_ssd, y_rwkv, y_att, mod, gain, wo, wr, br, w1, w3, w2)


def _pad_rows(a, rows):
    return jnp.pad(a, ((0, rows - a.shape[0]), (0, 0)))


def _reorder_w_in(w_in):
    widths = (640, 384, 12, 384, 384, 384, 64, 64, 128, 256, 256, 256)
    offs = [0]
    for w in widths:
        offs.append(offs[-1] + w)
    g = [w_in[..., offs[i]:offs[i + 1]] for i in range(len(widths))]
    xbc, z, dt, r, k, v, xw, xa, xg, qa, ka, va = g
    pad = jnp.zeros(w_in.shape[:-1] + (LANES - 12,), w_in.dtype)
    return jnp.concatenate([z, r, k, v, xw, xa, xg, qa, ka, va, xbc, dt, pad], axis=-1)


def _rope_tables(n_tokens):
    half = ATT_QK_DIM // 2
    quarter = half // 2
    tok = jnp.arange(n_tokens)
    row = (tok // GRID_W).astype(F32)
    col = (tok % GRID_W).astype(F32)
    inv = ROPE_BASE ** (-jnp.arange(quarter, dtype=F32) / quarter)
    lane = jnp.arange(ATT_WIDTH)
    within = lane % ATT_QK_DIM
    use_col = within >= half
    freq = inv[within % quarter]
    second = (within % half) >= quarter
    pos = jnp.where(use_col[None, :], col[:, None], row[:, None])
    ang = pos * freq[None, :]
    cos = jnp.cos(ang)
    sin = jnp.sin(ang)
    return jnp.stack([cos, jnp.where(second[None, :], 0.0, -sin), jnp.where(second[None, :], sin, 0.0)])


def kernel(x_prompt, x_sample, c, cache_attn_k, cache_attn_v, state_ssd, state_rwkv, c_ctx, ada_w, ada_b, norm1_g, norm2_g, w_in, ssd_conv_w, ssd_conv_b, ssd_dt_bias, ssd_a_log, ssd_d, ssd_norm_g, rwkv_w0, rwkv_w2, rwkv_a0, rwkv_a2, rwkv_g2, rwkv_kk, rwkv_ka, rwkv_rk, rwkv_ln_g, rwkv_ln_b, att_qnorm_g, att_knorm_g, att_lambda, att_onorm_g, w_out, moe_wg, moe_bg, moe_we, moe_be, moe_w1, moe_w3, moe_w2):
    n_ctx, s_ctx, _ = x_prompt.shape
    n_lat, s_lat, _ = x_sample.shape
    past = cache_attn_k.shape[2]

    cond = _pad_rows(jnp.concatenate([c_ctx[None, :], c], axis=0), MOD_ROWS)
    mod = _ada_mod(cond, ada_w, ada_b).reshape(DEPTH, MOD_ROWS, 6, D_MODEL)

    w_in_p = _reorder_w_in(w_in).astype(BF16)
    w_out_b = w_out.astype(BF16)
    w1_b, w3_b, w2_b = moe_w1.astype(BF16), moe_w3.astype(BF16), moe_w2.astype(BF16)
    w_route = jnp.concatenate(
        [moe_wg, moe_we, jnp.zeros((DEPTH, D_MODEL, LANES - N_GROUPS - N_EXPERTS), F32)], axis=-1)
    b_route = jnp.concatenate(
        [moe_bg, moe_be, jnp.zeros((DEPTH, LANES - N_GROUPS - N_EXPERTS), F32)], axis=-1)[:, None, :]
    rope_tab = _rope_tables(s_lat)
    cache_k = cache_attn_k.reshape(n_lat, DEPTH, past, ATT_WIDTH)
    cache_v = cache_attn_v.reshape(n_lat, DEPTH, past, ATT_WIDTH)
    zero_ssd = jnp.zeros((n_ctx, 1, 2, SSD_HEADS, SSD_HEAD_DIM, SSD_STATE), F32)
    zero_rwkv = jnp.zeros((n_ctx, 1, 2, RWKV_HEADS, RWKV_HEAD_DIM, RWKV_HEAD_DIM), F32)

    ctx_row = lambda i: 0
    lat_row = lambda i: 1 + i // (s_lat // TOKEN_TILE)

    xp = x_prompt.reshape(n_ctx * s_ctx, D_MODEL)
    xs = x_sample.reshape(n_lat * s_lat, D_MODEL)
    new_k, new_v, new_ssd, new_rwkv = [], [], [], []
    for l in range(DEPTH):
        conv = _pad_rows(jnp.concatenate([ssd_conv_w[l], ssd_conv_b[l][None, :]], axis=0), SUBLANES)
        ssd_vec = _pad_rows(jnp.stack([jnp.repeat(ssd_d[l], SSD_HEAD_DIM), ssd_norm_g[l]]), SUBLANES)
        dtp = _pad_rows(jnp.pad(jnp.stack([ssd_dt_bias[l].reshape(-1), ssd_a_log[l].reshape(-1)]),
                                ((0, 0), (0, LANES - 2 * SSD_HEADS))), SUBLANES)
        rwkv_vec = _pad_rows(jnp.stack([
            rwkv_kk[l], rwkv_ka[l], rwkv_rk[l].reshape(-1), rwkv_ln_g[l], rwkv_ln_b[l],
            rwkv_w0[l, 0], rwkv_w0[l, 1], rwkv_a0[l, 0], rwkv_a0[l, 1]]), 16)
        tile32 = lambda g: jnp.tile(g, ATT_WIDTH // ATT_QK_DIM)
        att_vec = _pad_rows(jnp.stack([tile32(att_qnorm_g[l]), tile32(att_knorm_g[l]),
                                       jnp.tile(att_onorm_g[l], ATT_WIDTH // ATT_V_DIM)]), SUBLANES)
        g1 = norm1_g[l][None, :]
        g2 = norm2_g[l][None, :]
        post_w = (mod, g2, w_out_b, w_route, b_route, w1_b, w3_b, w2_b)

        p = _in_proj(xp, mod, g1, w_in_p, ctx_row, l)
        y_ssd, st_ssd = _ssd_branch(p, s_ctx, n_ctx, conv, ssd_vec, dtp, zero_ssd, 0)
        y_rwkv, st_rwkv = _rwkv_branch(p, s_ctx, n_ctx, rwkv_vec, rwkv_w2, rwkv_a2, rwkv_g2, l, zero_rwkv, 0)
        y_att, k_att, v_att = _attn_branch(p, s_ctx, n_ctx, l, att_vec, att_lambda[l])
        xp = _post(xp, y_ssd, y_rwkv, y_att, *post_w, ctx_row, l)
        new_k.append(k_att.reshape(n_ctx, s_ctx, ATT_HEADS, 2, ATT_QK_DIM))
        new_v.append(v_att.reshape(n_ctx, s_ctx, ATT_HEADS, ATT_V_DIM))
        new_ssd.append(st_ssd)
        new_rwkv.append(st_rwkv)

        p = _in_proj(xs, mod, g1, w_in_p, lat_row, l)
        y_ssd, _ = _ssd_branch(p, s_lat, n_lat, conv, ssd_vec, dtp, state_ssd, l)
        y_rwkv, _ = _rwkv_branch(p, s_lat, n_lat, rwkv_vec, rwkv_w2, rwkv_a2, rwkv_g2, l, state_rwkv, l)
        y_att = _attn_branch(p, s_lat, n_lat, l, att_vec, att_lambda[l], rope_tab, cache_k, cache_v)
        xs = _post(xs, y_ssd, y_rwkv, y_att, *post_w, lat_row, l)

    return (xp.reshape(n_ctx, s_ctx, D_MODEL), xs.reshape(n_lat, s_lat, D_MODEL),
            jnp.stack(new_k, axis=1), jnp.stack(new_v, axis=1),
            jnp.stack(new_ssd, axis=1), jnp.stack(new_rwkv, axis=1))
```

```python
import functools
import math

import jax
import jax.numpy as jnp
from jax import lax
from jax.experimental import pallas as pl
from jax.experimental.pallas import tpu as pltpu

F32 = jnp.float32
BF16 = jnp.bfloat16
HIGHEST = lax.Precision.HIGHEST

D_MODEL = 1024
DEPTH = 2
GRID_W = 64
SSD_WIDTH = 384
SSD_HEADS = 6
SSD_GROUPS = 2
SSD_STATE = 64
SSD_HEAD_DIM = 64
SSD_CONV = 5
SSD_CONV_DIM = 640
SSD_CHUNK = 256
SSD_CONV_ROWS = 32
RWKV_WIDTH = 384
RWKV_HEADS = 6
RWKV_HEAD_DIM = 64
RWKV_CHUNK = 64
RWKV_SUB = 8
RWKV_PAIRS = RWKV_HEADS // 2
RWKV_GN_EPS = 64e-5
ATT_WIDTH = 256
ATT_HEADS = 4
ATT_QK_DIM = 32
ATT_V_DIM = 64
ATT_Q_BLOCK = 256
ROPE_BASE = 10000.0
N_GROUPS = 4
EXPERTS_PER_GROUP = 4
N_EXPERTS = 16
EXPERT_HIDDEN = 256
NORM_EPS = 1e-6
NORM_TERMS = 2

LANES = 128
SUBLANES = 8
TOKEN_TILE = 512
VMEM_LIMIT_BYTES = 56 * 1024 * 1024

P_WIDTH = 3328
P_Z, P_R, P_K, P_V = 0, 1, 2, 3
P_LORA, P_QA, P_KA, P_VA = 6, 7, 8, 9
P_XBC = 4
P_DT = 25
MOD_ROWS = 16


def _dot(a, b, precision=None):
    return lax.dot_general(a, b, (((1,), (0,)), ((), ())), precision=precision, preferred_element_type=F32)


def _dot_nt(a, b, precision=None):
    return lax.dot_general(a, b, (((1,), (1,)), ((), ())), precision=precision, preferred_element_type=F32)


def _dot_tn(a, b, precision=None):
    return lax.dot_general(a, b, (((0,), (0,)), ((), ())), precision=precision, preferred_element_type=F32)


def _bdot(a, b):
    return _dot(a.astype(BF16), b.astype(BF16))


def _bdot_nt(a, b):
    return _dot_nt(a.astype(BF16), b.astype(BF16))


def _bdot_tn(a, b):
    return _dot_tn(a.astype(BF16), b.astype(BF16))


def _bf16_terms(a, terms):
    parts = []
    for _ in range(terms):
        part = a.astype(BF16)
        a = a - part.astype(F32)
        parts.append(part)
    return parts


def _dot01(a, b01, terms=3):
    b = b01.astype(BF16)
    return sum(_dot(part, b) for part in _bf16_terms(a, terms))


def _dot_x3(a, b):
    a_hi, a_lo = _bf16_terms(a, 2)
    b_hi, b_lo = _bf16_terms(b, 2)
    return _dot(a_hi, b_hi) + (_dot(a_lo, b_hi) + _dot(a_hi, b_lo))


def _dot01_left(a01, b, terms=3):
    a = a01.astype(BF16)
    return sum(_dot(a, part) for part in _bf16_terms(b, terms))


def _head_sums(a, bd):
    slabs = [_dot01(a[:, s:s + LANES], bd, NORM_TERMS) for s in range(0, a.shape[1], LANES)]
    return jnp.concatenate(slabs, axis=1)


def _sigmoid(x):
    return 0.5 * jnp.tanh(0.5 * x) + 0.5


def _silu(x):
    return x * _sigmoid(x)


def _softplus(x):
    return jnp.maximum(x, 0.0) + jnp.log1p(jnp.exp(-jnp.abs(x)))


def _block_diag_ones(n, blk):
    r = lax.broadcasted_iota(jnp.int32, (n, n), 0) // blk
    c = lax.broadcasted_iota(jnp.int32, (n, n), 1) // blk
    return (r == c).astype(F32)


def _const_spec(shape):
    zeros = (0,) * len(shape)
    return pl.BlockSpec(shape, lambda *_: zeros, pipeline_mode=pl.Buffered(1))


def _layer_spec(shape, layer):
    idx = (layer,) + (0,) * len(shape)
    return pl.BlockSpec((None,) + tuple(shape), lambda *_: idx, pipeline_mode=pl.Buffered(1))


def _mod_spec(mod_row, layer):
    return pl.BlockSpec((None, 1, 6, D_MODEL), lambda i: (layer, mod_row(i), 0, 0))


def _params(n_axes=1):
    return pltpu.CompilerParams(dimension_semantics=("arbitrary",) * n_axes, vmem_limit_bytes=VMEM_LIMIT_BYTES)


def _ada_kernel(c_ref, w_ref, b_ref, o_ref):
    o_ref[0] = _dot(_silu(c_ref[...]), w_ref[0], HIGHEST) + b_ref[0]


def _ada_mod(cond, ada_w, ada_b):
    tn = 1024
    n = 6 * D_MODEL
    return pl.pallas_call(
        _ada_kernel,
        out_shape=jax.ShapeDtypeStruct((DEPTH, MOD_ROWS, n), F32),
        grid=(DEPTH, n // tn),
        in_specs=[
            pl.BlockSpec((MOD_ROWS, D_MODEL), lambda l, j: (0, 0)),
            pl.BlockSpec((1, D_MODEL, tn), lambda l, j: (l, 0, j)),
            pl.BlockSpec((1, 1, tn), lambda l, j: (l, 0, j)),
        ],
        out_specs=pl.BlockSpec((1, MOD_ROWS, tn), lambda l, j: (l, 0, j)),
        compiler_params=_params(2),
        name="ada_mod",
    )(cond, ada_w, ada_b.reshape(DEPTH, 1, n))


def _modulated_norm(x, gain, shift, scale):
    y = x * lax.rsqrt(jnp.mean(x * x, axis=-1, keepdims=True) + NORM_EPS)
    return y * gain * (1.0 + scale) + shift


def _in_proj_kernel(x_ref, mod_ref, g_ref, w_ref, o_ref):
    h = _modulated_norm(x_ref[...], g_ref[...], mod_ref[0, 0:1, :], mod_ref[0, 1:2, :]).astype(BF16)
    cw = 512
    for c0 in range(0, P_WIDTH, cw):
        c1 = min(c0 + cw, P_WIDTH)
        o_ref[:, c0:c1] = _dot(h, w_ref[:, c0:c1])


def _in_proj(x, mod, gain, w, mod_row, layer):
    n_tok = x.shape[0]
    return pl.pallas_call(
        _in_proj_kernel,
        out_shape=jax.ShapeDtypeStruct((n_tok, P_WIDTH), F32),
        grid=(n_tok // TOKEN_TILE,),
        in_specs=[
            pl.BlockSpec((TOKEN_TILE, D_MODEL), lambda i: (i, 0)),
            _mod_spec(mod_row, layer),
            _const_spec((1, D_MODEL)),
            _layer_spec((D_MODEL, P_WIDTH), layer),
        ],
        out_specs=pl.BlockSpec((TOKEN_TILE, P_WIDTH), lambda i: (i, 0)),
        compiler_params=_params(1),
        name="in_proj",
    )(x, mod, gain, w)


def _ssd_kernel(xbc_ref, z_ref, dt_ref, cw_ref, vec_ref, dtp_ref, init_ref, y_ref, fin_ref,
                pad_ref, xc_ref, dt_s, da_s, y_s, st_ref):
    seq = xbc_ref.shape[0]
    q = SSD_CHUNK
    n_chunks = seq // q
    rep = SSD_HEADS // SSD_GROUPS
    pad_ref[0:SUBLANES, :] = jnp.zeros((SUBLANES, SSD_CONV_DIM), F32)
    pad_ref[seq + SUBLANES:seq + 2 * SUBLANES, :] = jnp.zeros((SUBLANES, SSD_CONV_DIM), F32)
    pad_ref[SUBLANES:seq + SUBLANES, :] = xbc_ref[...]
    cr = SSD_CONV_ROWS
    for r0 in range(0, seq, cr):
        acc = jnp.zeros((cr, SSD_CONV_DIM), F32) + cw_ref[SSD_CONV:SSD_CONV + 1, :]
        for j in range(SSD_CONV):
            off = r0 + SUBLANES + j - SSD_CONV // 2
            acc = acc + cw_ref[j:j + 1, :] * pad_ref[off:off + cr, :]
        xc_ref[r0:r0 + cr, :] = _silu(acc)
    dt = _softplus(dt_ref[...] + dtp_ref[0:1, :])
    dt_s[...] = dt
    da_s[...] = dt * (-jnp.exp(dtp_ref[1:2, :]))

    row = lax.broadcasted_iota(jnp.int32, (q, q), 0)
    col = lax.broadcasted_iota(jnp.int32, (q, q), 1)
    keep = [col <= row, col >= row]
    tri = [k.astype(F32) for k in keep]
    wide = SSD_HEADS * SSD_HEAD_DIM
    src_c = lax.broadcasted_iota(jnp.int32, (LANES, wide), 0)
    head_c = lax.broadcasted_iota(jnp.int32, (LANES, wide), 1) // SSD_HEAD_DIM
    to_chan = [(src_c == d * SSD_HEADS + head_c).astype(F32) for d in range(2)]
    grp_ok = (src_c // SSD_STATE) == (head_c // rep)
    lane = lax.broadcasted_iota(jnp.int32, (q, LANES), 1)
    zero_pn = jnp.zeros((SSD_HEAD_DIM, SSD_STATE), F32)
    for d in range(2):
        rows_pn = []
        for h in range(SSD_HEADS):
            s = init_ref[0, 0, d, h]
            halves = [s, zero_pn] if h // rep == 0 else [zero_pn, s]
            rows_pn.append(jnp.concatenate(halves, axis=1))
        st_ref[d] = jnp.concatenate(rows_pn, axis=0).T

    def chunk_body(i, carry):
        for d in range(2):
            c = i if d == 0 else n_chunks - 1 - i
            rows = pl.ds(pl.multiple_of(c * q, q), q)
            last = q - 1 if d == 0 else 0
            da = da_s[rows, :]
            g = _dot01_left(tri[d], da)
            g_t = g.T
            chan = _dot01(jnp.concatenate([g, dt_s[rows, :]], axis=0), to_chan[d])
            g_chan = chan[0:q]
            dt_chan = chan[q:]
            tot_chan = g_chan[last:last + 1, :]
            xcc = xc_ref[rows, :]
            x_all = xcc[:, 0:wide] * dt_chan
            x_dec = x_all * jnp.exp(tot_chan - g_chan)
            bw = xcc[:, wide:wide + LANES].astype(BF16)
            cw = xcc[:, wide + LANES:wide + 2 * LANES]
            cwb = cw.astype(BF16)
            cb = [_dot_nt(jnp.where(lane < SSD_STATE, cw, 0.0).astype(BF16), bw),
                  _dot_nt(jnp.where(lane < SSD_STATE, 0.0, cw).astype(BF16), bw)]
            tiles = []
            for h in range(SSD_HEADS):
                cidx = d * SSD_HEADS + h
                diff = g[:, cidx:cidx + 1] - g_t[cidx:cidx + 1, :]
                lm = jnp.exp(jnp.where(keep[d], diff, -jnp.inf))
                tiles.append((cb[h // rep] * lm).astype(BF16))
            y_diag = []
            for pr in range(SSD_HEADS // 2):
                x_pair = x_all[:, pr * LANES:(pr + 1) * LANES]
                x_bd = jnp.concatenate([jnp.where(lane < SSD_HEAD_DIM, x_pair, 0.0),
                                        jnp.where(lane < SSD_HEAD_DIM, 0.0, x_pair)], axis=0).astype(BF16)
                y_diag.append(_dot(jnp.concatenate(tiles[2 * pr:2 * pr + 2], axis=1), x_bd))
            st = st_ref[d]
            y_s[d, rows, :] = jnp.concatenate(y_diag, axis=1) + _dot(cwb, st.astype(BF16)) * jnp.exp(g_chan)
            st_new = jnp.exp(tot_chan) * st + _dot_tn(bw, x_dec.astype(BF16))
            st_ref[d] = jnp.where(grp_ok, st_new, 0.0)
        return carry

    lax.fori_loop(0, n_chunks, chunk_body, 0)
    for d in range(2):
        st = st_ref[d].T
        for h in range(SSD_HEADS):
            grp = h // rep
            fin_ref[0, d, h] = st[h * SSD_HEAD_DIM:(h + 1) * SSD_HEAD_DIM, grp * SSD_STATE:(grp + 1) * SSD_STATE]

    y = y_s[0] + y_s[1] + vec_ref[0:1, :] * xc_ref[:, 0:SSD_WIDTH]
    y = y * _silu(z_ref[...])
    y = y * lax.rsqrt(jnp.mean(y * y, axis=-1, keepdims=True) + NORM_EPS)
    y_ref[...] = y * vec_ref[1:2, :]


def _ssd_branch(p, seq, n_batch, conv, vec, dtp, init, init_layer):
    st_shape = (n_batch, 2, SSD_HEADS, SSD_HEAD_DIM, SSD_STATE)
    return pl.pallas_call(
        _ssd_kernel,
        out_shape=(jax.ShapeDtypeStruct((n_batch * seq, SSD_WIDTH), F32), jax.ShapeDtypeStruct(st_shape, F32)),
        grid=(n_batch,),
        in_specs=[
            pl.BlockSpec((seq, SSD_CONV_DIM), lambda b: (b, P_XBC)),
            pl.BlockSpec((seq, SSD_WIDTH), lambda b: (b, P_Z)),
            pl.BlockSpec((seq, LANES), lambda b: (b, P_DT)),
            _const_spec((SUBLANES, SSD_CONV_DIM)),
            _const_spec((SUBLANES, SSD_WIDTH)),
            _const_spec((SUBLANES, LANES)),
            pl.BlockSpec((1, 1) + st_shape[1:], lambda b: (b, init_layer, 0, 0, 0, 0)),
        ],
        out_specs=(
            pl.BlockSpec((seq, SSD_WIDTH), lambda b: (b, 0)),
            pl.BlockSpec((1,) + st_shape[1:], lambda b: (b, 0, 0, 0, 0)),
        ),
        scratch_shapes=[
            pltpu.VMEM((seq + 2 * SUBLANES, SSD_CONV_DIM), F32),
            pltpu.VMEM((seq, SSD_CONV_DIM), F32),
            pltpu.VMEM((seq, LANES), F32),
            pltpu.VMEM((seq, LANES), F32),
            pltpu.VMEM((2, seq, SSD_WIDTH), F32),
            pltpu.VMEM((2, 2 * SSD_STATE, SSD_HEADS * SSD_HEAD_DIM), F32),
        ],
        compiler_params=_params(1),
        name="ssd",
    )(p, p, p, conv, vec, dtp, init)


def _rwkv_kernel(r_ref, k_ref, v_ref, lora_ref, vec_ref, w2_ref, a2_ref, g2_ref, init_ref, y_ref, fin_ref,
                 kkn_s, lw_s, kd_s, kka_s, y_s, st_ref):
    seq = r_ref.shape[0]
    t = RWKV_CHUNK
    n_chunks = seq // t
    hd = RWKV_HEAD_DIM
    n2 = 2 * t
    n4 = 4 * t
    r = r_ref[...]
    k = k_ref[...]
    lora = lora_ref[...]
    xw = lora[:, 0:64]
    xa = lora[:, 64:128]
    xg = lora[:, 128:256]
    bd = _block_diag_ones(LANES, hd)
    kkp = k * vec_ref[0:1, :]
    kss = _head_sums(kkp * kkp, bd)
    kkn = kkp * lax.rsqrt(jnp.maximum(kss, 1e-24))
    kkn_s[...] = kkn
    tw = jnp.tanh(xw)
    zero_hh = jnp.zeros((hd, hd), F32)
    for d in range(2):
        wl = vec_ref[5 + d:6 + d, :] + _bdot(tw, w2_ref[d])
        lw_s[d] = -math.exp(-0.5) * _sigmoid(wl)
        a = _sigmoid(vec_ref[7 + d:8 + d, :] + _bdot(xa, a2_ref[d]))
        kd_s[d] = k * (1.0 + (a - 1.0) * vec_ref[1:2, :])
        kka_s[d] = kkn * a
        for p in range(RWKV_PAIRS):
            st_ref[d, p] = jnp.concatenate([
                jnp.concatenate([init_ref[0, 0, d, 2 * p], zero_hh], axis=1),
                jnp.concatenate([zero_hh, init_ref[0, 0, d, 2 * p + 1]], axis=1)], axis=0)

    ri = lax.broadcasted_iota(jnp.int32, (n4, n4), 0)
    ci = lax.broadcasted_iota(jnp.int32, (n4, n4), 1)
    rt = ri % t
    ct = ci % t
    top = ri < n2
    keep = [(top & (ct < rt)) | (~top & (ct <= rt)), (top & (ct > rt)) | (~top & (ct >= rt))]
    si = lax.broadcasted_iota(jnp.int32, (n2, n2), 0) // RWKV_SUB
    sj = lax.broadcasted_iota(jnp.int32, (n2, n2), 1) // RWKV_SUB
    sub_diag = si == sj
    tr = lax.broadcasted_iota(jnp.int32, (t, t), 0)
    tc = lax.broadcasted_iota(jnp.int32, (t, t), 1)
    tri = [(tc <= tr).astype(F32), (tc >= tr).astype(F32)]
    lo = lax.broadcasted_iota(jnp.int32, (t, LANES), 1) < hd

    def block_diag(x):
        return jnp.concatenate([jnp.where(lo, x, 0.0), jnp.where(lo, 0.0, x)], axis=0)

    def chunk_body(i, carry):
        ch = []
        for d in range(2):
            c = i if d == 0 else n_chunks - 1 - i
            rows = pl.ds(pl.multiple_of(c * t, t), t)
            lw = lw_s[d, rows, :]
            g = _dot01_left(tri[d], lw)
            e_g = jnp.exp(g)
            e_ng = jnp.exp(-g)
            kap = kkn_s[rows, :] * jnp.exp(g - lw)
            rr = r_ref[rows, :] * e_g
            bb = kka_s[d, rows, :] * e_ng
            kk = kd_s[d, rows, :] * e_ng
            vv = v_ref[rows, :]
            e_tot = jnp.exp(jnp.sum(lw, axis=0, keepdims=True))
            for p in range(RWKV_PAIRS):
                sl = slice(p * LANES, (p + 1) * LANES)
                ch.append(dict(
                    d=d, p=p, rows=rows,
                    lq=jnp.concatenate([block_diag(kap[:, sl]), block_diag(rr[:, sl])], axis=0).astype(BF16),
                    rq=jnp.concatenate([block_diag(bb[:, sl]), block_diag(kk[:, sl])], axis=0).astype(BF16),
                    vb=block_diag(vv[:, sl]),
                    s0=st_ref[d, p], etot=e_tot[:, sl]))
        for x in ch:
            x["mm"] = jnp.where(keep[x["d"]], _dot_nt(x["lq"], x["rq"]), 0.0)
            x["c0"] = _bdot_nt(x["lq"], x["s0"])
        for x in ch:
            x["base"] = x["c0"] + _bdot(x["mm"][:, n2:], x["vb"])
            ab = x["mm"][0:n2, 0:n2]
            x["nd"] = jnp.where(sub_diag, ab, 0.0)
            x["rhs"] = jnp.concatenate([jnp.where(sub_diag, 0.0, ab), x["base"][0:n2]], axis=1)
        for x in ch:
            x["nd2"] = _bdot(x["nd"], x["nd"])
            x["y"] = x["rhs"] - _bdot(x["nd"], x["rhs"])
        for x in ch:
            x["nd4"] = _bdot(x["nd2"], x["nd2"])
            x["y"] = x["y"] + _bdot(x["nd2"], x["y"])
        for x in ch:
            x["y"] = x["y"] + _bdot(x["nd4"], x["y"])
            x["m"] = x["y"][:, 0:n2]
            x["x"] = x["y"][:, n2:]
        for x in ch:
            x["m2"] = _bdot(x["m"], x["m"])
            x["x"] = x["x"] - _bdot(x["m"], x["x"])
        for x in ch:
            x["m4"] = _bdot(x["m2"], x["m2"])
            x["x"] = x["x"] + _bdot(x["m2"], x["x"])
        for x in ch:
            x["x"] = x["x"] + _bdot(x["m4"], x["x"])
            x["zu"] = -x["x"]
        for x in ch:
            yb = x["base"][n2:] + _bdot(x["mm"][n2:, 0:n2], x["zu"])
            x["yo"] = yb[0:t] + yb[t:]
            uv = jnp.concatenate([x["zu"], x["vb"]], axis=0)
            st_ref[x["d"], x["p"]] = (x["s0"] + _bdot_tn(uv, x["rq"])) * x["etot"]
        for d in range(2):
            rows = ch[d * RWKV_PAIRS]["rows"]
            y_s[d, rows, :] = jnp.concatenate([x["yo"] for x in ch if x["d"] == d], axis=1)
        return carry

    lax.fori_loop(0, n_chunks, chunk_body, 0)
    for d in range(2):
        for p in range(RWKV_PAIRS):
            s = st_ref[d, p]
            fin_ref[0, d, 2 * p] = s[0:hd, 0:hd]
            fin_ref[0, d, 2 * p + 1] = s[hd:, hd:]

    yf = y_s[0] + y_s[1]
    inv = 1.0 / hd
    mu = _head_sums(yf, bd) * inv
    yc = yf - mu
    var = _head_sums(yc * yc, bd) * inv
    yn = yc * lax.rsqrt(var + RWKV_GN_EPS) * vec_ref[3:4, :] + vec_ref[4:5, :]
    v = v_ref[...]
    bonus = _head_sums(r * k * vec_ref[2:3, :], bd) * v
    gate = _bdot(_sigmoid(xg), g2_ref[...])
    y_ref[...] = (yn + bonus) * gate


def _rwkv_branch(p, seq, n_batch, vec, w2, a2, g2, layer, init, init_layer):
    st_shape = (n_batch, 2, RWKV_HEADS, RWKV_HEAD_DIM, RWKV_HEAD_DIM)
    wide = lambda col: pl.BlockSpec((seq, RWKV_WIDTH), lambda b: (b, col))
    return pl.pallas_call(
        _rwkv_kernel,
        out_shape=(jax.ShapeDtypeStruct((n_batch * seq, RWKV_WIDTH), F32), jax.ShapeDtypeStruct(st_shape, F32)),
        grid=(n_batch,),
        in_specs=[
            wide(P_R), wide(P_K), wide(P_V),
            pl.BlockSpec((seq, 256), lambda b: (b, P_LORA)),
            _const_spec((16, RWKV_WIDTH)),
            _layer_spec((2, 64, RWKV_WIDTH), layer),
            _layer_spec((2, 64, RWKV_WIDTH), layer),
            _layer_spec((128, RWKV_WIDTH), layer),
            pl.BlockSpec((1, 1) + st_shape[1:], lambda b: (b, init_layer, 0, 0, 0, 0)),
        ],
        out_specs=(
            pl.BlockSpec((seq, RWKV_WIDTH), lambda b: (b, 0)),
            pl.BlockSpec((1,) + st_shape[1:], lambda b: (b, 0, 0, 0, 0)),
        ),
        scratch_shapes=[pltpu.VMEM((seq, RWKV_WIDTH), F32)]
        + [pltpu.VMEM((2, seq, RWKV_WIDTH), F32)] * 4
        + [pltpu.VMEM((2, RWKV_PAIRS, 2 * RWKV_HEAD_DIM, 2 * RWKV_HEAD_DIM), F32)],
        compiler_params=_params(1),
        name="rwkv",
    )(p, p, p, p, vec, w2, a2, g2, init)


def _group_rmsnorm(x, bd, width, gain):
    ms = _dot01(x * x, bd, NORM_TERMS) * (1.0 / width)
    return x * lax.rsqrt(ms + NORM_EPS) * gain


def _rope(x, tab_ref):
    half = 8
    x_next = pltpu.roll(x, ATT_WIDTH - half, 1)
    x_prev = pltpu.roll(x, half, 1)
    return x * tab_ref[0] + x_next * tab_ref[1] + x_prev * tab_ref[2]


def _attn_kernel(*refs, layer, latent):
    if latent:
        (q_ref, k_ref, v_ref, gv_ref, lam_ref, tab_ref, ck_ref, cv_ref, y_ref, ka_s, va_s, qn_s) = refs
    else:
        (q_ref, k_ref, v_ref, gv_ref, lam_ref, y_ref, kn_ref, vo_ref, ka_s, va_s, qn_s) = refs
    seq = q_ref.shape[0]
    bd32 = _block_diag_ones(ATT_WIDTH, ATT_QK_DIM)
    qn = _group_rmsnorm(q_ref[...], bd32, ATT_QK_DIM, gv_ref[0:1, :])
    kn = _group_rmsnorm(k_ref[...], bd32, ATT_QK_DIM, gv_ref[1:2, :])
    v = v_ref[...]
    if latent:
        past = ck_ref.shape[2]
        qn = _rope(qn, tab_ref)
        kn = _rope(kn, tab_ref)
        ka_s[0:past, :] = ck_ref[0, 0].astype(BF16)
        va_s[0:past, :] = cv_ref[0, 0].astype(BF16)
    else:
        past = 0
        kn_ref[...] = kn
        vo_ref[...] = v
    ka_s[past:past + seq, :] = kn.astype(BF16)
    va_s[past:past + seq, :] = v.astype(BF16)
    qn_s[...] = qn

    lam_init = 0.8 - 0.6 * math.exp(-0.3 * layer)
    lv = lam_ref[...]
    lam = (jnp.exp(jnp.sum(lv[0:1, :] * lv[1:2, :], axis=-1, keepdims=True))
           - jnp.exp(jnp.sum(lv[2:3, :] * lv[3:4, :], axis=-1, keepdims=True)) + lam_init)
    scale = ATT_QK_DIM ** -0.5
    bd64 = _block_diag_ones(LANES, ATT_V_DIM)
    lane = lax.broadcasted_iota(jnp.int32, (1, LANES), 1)
    tq = min(ATT_Q_BLOCK, seq)

    def q_block(i, carry):
        rows = pl.ds(pl.multiple_of(i * tq, tq), tq)
        outs = []
        for pair in range(ATT_HEADS // 2):
            cols = slice(pair * LANES, (pair + 1) * LANES)
            q_pair = qn_s[rows, cols]
            k_pair = ka_s[:, cols]
            v_pair = va_s[:, cols]
            o_heads = []
            for hh in range(2):
                wgt = None
                for m in range(2):
                    lo = hh * 2 * ATT_QK_DIM + m * ATT_QK_DIM
                    sel = (lane >= lo) & (lane < lo + ATT_QK_DIM)
                    qm = jnp.where(sel, q_pair * scale, 0.0).astype(BF16)
                    sc = _dot_nt(qm, k_pair)
                    e = jnp.exp(sc - jnp.max(sc, axis=-1, keepdims=True))
                    inv = 1.0 / jnp.sum(e, axis=-1, keepdims=True)
                    wgt = e * inv if m == 0 else wgt - e * (lam * inv)
                o_heads.append(_dot(wgt.astype(BF16), v_pair))
            o = jnp.where(lane < ATT_V_DIM, o_heads[0], o_heads[1])
            ms = _dot01(o * o, bd64, NORM_TERMS) * (1.0 / ATT_V_DIM)
            outs.append(o * lax.rsqrt(ms + NORM_EPS) * gv_ref[2:3, cols] * (1.0 - lam_init))
        y_ref[rows, :] = jnp.concatenate(outs, axis=1)
        return carry

    lax.fori_loop(0, seq // tq, q_block, 0)


def _attn_branch(p, seq, n_batch, layer, gv, lam, rope_tab=None, cache_k=None, cache_v=None):
    latent = cache_k is not None
    blk = lambda col: pl.BlockSpec((seq, ATT_WIDTH), lambda b: (b, col))
    in_specs = [blk(P_QA), blk(P_KA), blk(P_VA), _const_spec((SUBLANES, ATT_WIDTH)), _const_spec((4, ATT_QK_DIM))]
    args = [p, p, p, gv, lam]
    seq_out = jax.ShapeDtypeStruct((n_batch * seq, ATT_WIDTH), F32)
    out_blk = pl.BlockSpec((seq, ATT_WIDTH), lambda b: (b, 0))
    if latent:
        past = cache_k.shape[2]
        in_specs += [
            _const_spec((3, seq, ATT_WIDTH)),
            pl.BlockSpec((1, 1, past, ATT_WIDTH), lambda b: (b, layer, 0, 0)),
            pl.BlockSpec((1, 1, past, ATT_WIDTH), lambda b: (b, layer, 0, 0)),
        ]
        args += [rope_tab, cache_k, cache_v]
        out_shape, out_specs = seq_out, out_blk
    else:
        past = 0
        out_shape, out_specs = (seq_out,) * 3, (out_blk,) * 3
    return pl.pallas_call(
        functools.partial(_attn_kernel, layer=layer, latent=latent),
        out_shape=out_shape,
        grid=(n_batch,),
        in_specs=in_specs,
        out_specs=out_specs,
        scratch_shapes=[
            pltpu.VMEM((past + seq, ATT_WIDTH), BF16),
            pltpu.VMEM((past + seq, ATT_WIDTH), BF16),
            pltpu.VMEM((seq, ATT_WIDTH), F32),
        ],
        compiler_params=_params(1),
        name="attn",
    )(*args)


def _route(logits):
    lane = lax.broadcasted_iota(jnp.int32, logits.shape, 1)
    big = jnp.int32(1 << 20)
    neg = -jnp.inf
    gmask = lane < N_GROUPS
    gl = jnp.where(gmask, logits, neg)
    gmax = jnp.max(gl, axis=-1, keepdims=True)
    gsum = jnp.sum(jnp.where(gmask, jnp.exp(gl - gmax), 0.0), axis=-1, keepdims=True)
    g_val = 1.0 / gsum
    g_idx = jnp.min(jnp.where(gmask & (gl == gmax), lane, big), axis=-1, keepdims=True)
    e_lane = lane - N_GROUPS
    sel = (e_lane >= 0) & (e_lane < N_EXPERTS) & ((e_lane // EXPERTS_PER_GROUP) == g_idx)
    l1 = jnp.max(jnp.where(sel, logits, neg), axis=-1, keepdims=True)
    i1 = jnp.min(jnp.where(sel & (logits == l1), lane, big), axis=-1, keepdims=True)
    sel2 = sel & (lane != i1)
    l2 = jnp.max(jnp.where(sel2, logits, neg), axis=-1, keepdims=True)
    i2 = jnp.min(jnp.where(sel2 & (logits == l2), lane, big), axis=-1, keepdims=True)
    e2 = jnp.exp(l2 - l1)
    w1 = 1.0 / (1.0 + e2)
    w2 = e2 * w1
    return jnp.where(lane == i1, g_val * w1, 0.0) + jnp.where(lane == i2, g_val * w2, 0.0)


def _post_kernel(x_ref, ys_ref, yr_ref, ya_ref, mod_ref, g_ref, wo_ref, wr_ref, br_ref, w1_ref, w3_ref, w2_ref, o_ref):
    mix = _dot(ys_ref[...].astype(BF16), wo_ref[0:SSD_WIDTH, :])
    mix = mix + _dot(yr_ref[...].astype(BF16), wo_ref[SSD_WIDTH:SSD_WIDTH + RWKV_WIDTH, :])
    mix = mix + _dot(ya_ref[...].astype(BF16), wo_ref[SSD_WIDTH + RWKV_WIDTH:, :])
    x = x_ref[...] + mod_ref[0, 2:3, :] * mix
    h = _modulated_norm(x, g_ref[...], mod_ref[0, 3:4, :], mod_ref[0, 4:5, :])
    gate = _route(_dot_x3(h, wr_ref[...]) + br_ref[...])
    hb = h.astype(BF16)
    acc = jnp.zeros(x.shape, F32)
    for e in range(N_EXPERTS):
        a = _dot(hb, w1_ref[e])
        b = _dot(hb, w3_ref[e])
        hid = _silu(a) * b * gate[:, N_GROUPS + e:N_GROUPS + e + 1]
        acc = acc + _dot(hid.astype(BF16), w2_ref[e])
    o_ref[...] = x + mod_ref[0, 5:6, :] * acc


def _post(x, y_ssd, y_rwkv, y_att, mod, gain, wo, wr, br, w1, w3, w2, mod_row, layer):
    n_tok = x.shape[0]
    tok = lambda width: pl.BlockSpec((TOKEN_TILE, width), lambda i: (i, 0))
    return pl.pallas_call(
        _post_kernel,
        out_shape=jax.ShapeDtypeStruct((n_tok, D_MODEL), F32),
        grid=(n_tok // TOKEN_TILE,),
        in_specs=[
            tok(D_MODEL), tok(SSD_WIDTH), tok(RWKV_WIDTH), tok(ATT_WIDTH),
            _mod_spec(mod_row, layer),
            _const_spec((1, D_MODEL)),
            _layer_spec((D_MODEL, D_MODEL), layer),
            _layer_spec((D_MODEL, LANES), layer),
            _layer_spec((1, LANES), layer),
            _layer_spec((N_EXPERTS, D_MODEL, EXPERT_HIDDEN), layer),
            _layer_spec((N_EXPERTS, D_MODEL, EXPERT_HIDDEN), layer),
            _layer_spec((N_EXPERTS, EXPERT_HIDDEN, D_MODEL), layer),
        ],
        out_specs=tok(D_MODEL),
        compiler_params=_params(1),
        name="post",
    )(x, y_ssd, y_rwkv, y_att, mod, gain, wo, wr, br, w1, w3, w2)


def _pad_rows(a, rows):
    return jnp.pad(a, ((0, rows - a.shape[0]), (0, 0)))


def _reorder_w_in(w_in):
    widths = (640, 384, 12, 384, 384, 384, 64, 64, 128, 256, 256, 256)
    offs = [0]
    for w in widths:
        offs.append(offs[-1] + w)
    g = [w_in[..., offs[i]:offs[i + 1]] for i in range(len(widths))]
    xbc, z, dt, r, k, v, xw, xa, xg, qa, ka, va = g
    pad = jnp.zeros(w_in.shape[:-1] + (LANES - 12,), w_in.dtype)
    return jnp.concatenate([z, r, k, v, xw, xa, xg, qa, ka, va, xbc, dt, pad], axis=-1)


def _rope_tables(n_tokens):
    half = ATT_QK_DIM // 2
    quarter = half // 2
    tok = jnp.arange(n_tokens)
    row = (tok // GRID_W).astype(F32)
    col = (tok % GRID_W).astype(F32)
    inv = ROPE_BASE ** (-jnp.arange(quarter, dtype=F32) / quarter)
    lane = jnp.arange(ATT_WIDTH)
    within = lane % ATT_QK_DIM
    use_col = within >= half
    freq = inv[within % quarter]
    second = (within % half) >= quarter
    pos = jnp.where(use_col[None, :], col[:, None], row[:, None])
    ang = pos * freq[None, :]
    cos = jnp.cos(ang)
    sin = jnp.sin(ang)
    return jnp.stack([cos, jnp.where(second[None, :], 0.0, -sin), jnp.where(second[None, :], sin, 0.0)])


def kernel(x_prompt, x_sample, c, cache_attn_k, cache_attn_v, state_ssd, state_rwkv, c_ctx, ada_w, ada_b, norm1_g, norm2_g, w_in, ssd_conv_w, ssd_conv_b, ssd_dt_bias, ssd_a_log, ssd_d, ssd_norm_g, rwkv_w0, rwkv_w2, rwkv_a0, rwkv_a2, rwkv_g2, rwkv_kk, rwkv_ka, rwkv_rk, rwkv_ln_g, rwkv_ln_b, att_qnorm_g, att_knorm_g, att_lambda, att_onorm_g, w_out, moe_wg, moe_bg, moe_we, moe_be, moe_w1, moe_w3, moe_w2):
    n_ctx, s_ctx, _ = x_prompt.shape
    n_lat, s_lat, _ = x_sample.shape
    past = cache_attn_k.shape[2]

    cond = _pad_rows(jnp.concatenate([c_ctx[None, :], c], axis=0), MOD_ROWS)
    mod = _ada_mod(cond, ada_w, ada_b).reshape(DEPTH, MOD_ROWS, 6, D_MODEL)

    w_in_p = _reorder_w_in(w_in).astype(BF16)
    w_out_b = w_out.astype(BF16)
    w1_b, w3_b, w2_b = moe_w1.astype(BF16), moe_w3.astype(BF16), moe_w2.astype(BF16)
    w_route = jnp.concatenate(
        [moe_wg, moe_we, jnp.zeros((DEPTH, D_MODEL, LANES - N_GROUPS - N_EXPERTS), F32)], axis=-1)
    b_route = jnp.concatenate(
        [moe_bg, moe_be, jnp.zeros((DEPTH, LANES - N_GROUPS - N_EXPERTS), F32)], axis=-1)[:, None, :]
    rope_tab = _rope_tables(s_lat)
    cache_k = cache_attn_k.reshape(n_lat, DEPTH, past, ATT_WIDTH)
    cache_v = cache_attn_v.reshape(n_lat, DEPTH, past, ATT_WIDTH)
    zero_ssd = jnp.zeros((n_ctx, 1, 2, SSD_HEADS, SSD_HEAD_DIM, SSD_STATE), F32)
    zero_rwkv = jnp.zeros((n_ctx, 1, 2, RWKV_HEADS, RWKV_HEAD_DIM, RWKV_HEAD_DIM), F32)

    ctx_row = lambda i: 0
    lat_row = lambda i: 1 + i // (s_lat // TOKEN_TILE)

    xp = x_prompt.reshape(n_ctx * s_ctx, D_MODEL)
    xs = x_sample.reshape(n_lat * s_lat, D_MODEL)
    new_k, new_v, new_ssd, new_rwkv = [], [], [], []
    for l in range(DEPTH):
        conv = _pad_rows(jnp.concatenate([ssd_conv_w[l], ssd_conv_b[l][None, :]], axis=0), SUBLANES)
        ssd_vec = _pad_rows(jnp.stack([jnp.repeat(ssd_d[l], SSD_HEAD_DIM), ssd_norm_g[l]]), SUBLANES)
        dtp = _pad_rows(jnp.pad(jnp.stack([ssd_dt_bias[l].reshape(-1), ssd_a_log[l].reshape(-1)]),
                                ((0, 0), (0, LANES - 2 * SSD_HEADS))), SUBLANES)
        rwkv_vec = _pad_rows(jnp.stack([
            rwkv_kk[l], rwkv_ka[l], rwkv_rk[l].reshape(-1), rwkv_ln_g[l], rwkv_ln_b[l],
            rwkv_w0[l, 0], rwkv_w0[l, 1], rwkv_a0[l, 0], rwkv_a0[l, 1]]), 16)
        tile32 = lambda g: jnp.tile(g, ATT_WIDTH // ATT_QK_DIM)
        att_vec = _pad_rows(jnp.stack([tile32(att_qnorm_g[l]), tile32(att_knorm_g[l]),
                                       jnp.tile(att_onorm_g[l], ATT_WIDTH // ATT_V_DIM)]), SUBLANES)
        g1 = norm1_g[l][None, :]
        g2 = norm2_g[l][None, :]
        post_w = (mod, g2, w_out_b, w_route, b_route, w1_b, w3_b, w2_b)

        p = _in_proj(xp, mod, g1, w_in_p, ctx_row, l)
        y_ssd, st_ssd = _ssd_branch(p, s_ctx, n_ctx, conv, ssd_vec, dtp, zero_ssd, 0)
        y_rwkv, st_rwkv = _rwkv_branch(p, s_ctx, n_ctx, rwkv_vec, rwkv_w2, rwkv_a2, rwkv_g2, l, zero_rwkv, 0)
        y_att, k_att, v_att = _attn_branch(p, s_ctx, n_ctx, l, att_vec, att_lambda[l])
        xp = _post(xp, y_ssd, y_rwkv, y_att, *post_w, ctx_row, l)
        new_k.append(k_att.reshape(n_ctx, s_ctx, ATT_HEADS, 2, ATT_QK_DIM))
        new_v.append(v_att.reshape(n_ctx, s_ctx, ATT_HEADS, ATT_V_DIM))
        new_ssd.append(st_ssd)
        new_rwkv.append(st_rwkv)

        p = _in_proj(xs, mod, g1, w_in_p, lat_row, l)
        y_ssd, _ = _ssd_branch(p, s_lat, n_lat, conv, ssd_vec, dtp, state_ssd, l)
        y_rwkv, _ = _rwkv_branch(p, s_lat, n_lat, rwkv_vec, rwkv_w2, rwkv_a2, rwkv_g2, l, state_rwkv, l)
        y_att = _attn_branch(p, s_lat, n_lat, l, att_vec, att_lambda[l], rope_tab, cache_k, cache_v)
        xs = _post(xs, y_ssd, y_rwkv, y_att, *post_w, lat_row, l)

    return (xp.reshape(n_ctx, s_ctx, D_MODEL), xs.reshape(n_lat, s_lat, D_MODEL),
            jnp.stack(new_k, axis=1), jnp.stack(new_v, axis=1),
            jnp.stack(new_ssd, axis=1), jnp.stack(new_rwkv, axis=1))
```
